```python
import math
import jax, jax.numpy as jnp
from jax import lax
import numpy as np

D_MODEL = 2048
BATCH = 2
SEQ = 8192
DEPTH = 4

N_DIFF_HEADS = 4
DIFF_HEAD_DIM = 128
DIFF_V_DIM = 2 * DIFF_HEAD_DIM
N_DIFF_MAPS = 2 * N_DIFF_HEADS
Q_BLOCK = 128
N_RET_HEADS = 4
RET_KEY_DIM = 256
RET_V_DIM = 256
RET_CHUNK = 128
ROPE_BASE = 10000.0
NUM_BUCKETS = 32
MAX_DISTANCE = 128
POOL_WINDOWS = (2, 4, 8, 16)
N_POOL_GROUPS = 4
POOL_GROUP_DIM = D_MODEL // N_POOL_GROUPS
D_FF = 5632
CONV_WIDTH = 3
EPS = 1e-6
NEG_INF = -1e30
DIFF_QK_WIDTH = N_DIFF_HEADS * 2 * DIFF_HEAD_DIM
DIFF_V_WIDTH = N_DIFF_HEADS * DIFF_V_DIM
RET_QK_WIDTH = N_RET_HEADS * RET_KEY_DIM
RET_V_WIDTH = N_RET_HEADS * RET_V_DIM
IN_SPLITS = (DIFF_QK_WIDTH, DIFF_QK_WIDTH, DIFF_V_WIDTH,
             RET_QK_WIDTH, RET_QK_WIDTH, RET_V_WIDTH, RET_V_WIDTH)
IN_WIDTH = sum(IN_SPLITS)
MIX_WIDTH = DIFF_V_WIDTH + RET_V_WIDTH
N_PAR_LAYERS = (DEPTH + 1) // 2
N_POOL_LAYERS = DEPTH // 2

kernel_name = "hybrid_diffattn_retention_pool_convffn"


def rms_norm(x, g):
    xf = x.astype(jnp.float32)
    y = xf * lax.rsqrt(jnp.mean(xf * xf, axis=-1, keepdims=True) + EPS)
    return (y * g.astype(jnp.float32)).astype(x.dtype)


def t5_bucket(dist):
    max_exact = NUM_BUCKETS // 2
    is_small = dist < max_exact
    df = jnp.maximum(dist, 1).astype(jnp.float32)
    large = max_exact + (jnp.log(df / max_exact) / math.log(MAX_DISTANCE / max_exact)
                         * (NUM_BUCKETS - max_exact)).astype(jnp.int32)
    large = jnp.minimum(large, NUM_BUCKETS - 1)
    return jnp.where(is_small, dist, large)


def diff_attention(q, k, v, rel_bias, lam, subln, lam_init):
    B, S, _ = q.shape
    H, M, dh = N_DIFF_HEADS, N_DIFF_MAPS, DIFF_HEAD_DIM
    q = q.reshape(B, S, M, dh).transpose(0, 2, 1, 3).astype(jnp.float32) * (dh ** -0.5)
    k = k.reshape(B, S, M, dh).transpose(0, 2, 1, 3).astype(jnp.float32)
    v = v.reshape(B, S, H, DIFF_V_DIM).transpose(0, 2, 1, 3).astype(jnp.float32)
    n_blk = S // Q_BLOCK
    q_blocks = jnp.moveaxis(q.reshape(B, M, n_blk, Q_BLOCK, dh), 2, 0)
    k_pos = jnp.arange(S)

    def block(args):
        blk, qb = args
        q_pos = blk * Q_BLOCK + jnp.arange(Q_BLOCK)
        dist = q_pos[:, None] - k_pos[None, :]
        bias = rel_bias[t5_bucket(jnp.maximum(dist, 0))]
        s = jnp.einsum('bmqd,bmkd->bmqk', qb, k) + jnp.moveaxis(bias, -1, 0).astype(jnp.float32)
        s = jnp.where(dist[None, None] >= 0, s, NEG_INF)
        p = jax.nn.softmax(s, axis=-1).reshape(B, H, 2, Q_BLOCK, S)
        a = p[:, :, 0] - lam * p[:, :, 1]
        return jnp.einsum('bhqk,bhkd->bhqd', a, v)

    o = lax.map(block, (jnp.arange(n_blk), q_blocks))
    o = jnp.moveaxis(o, 0, 2).reshape(B, H, S, DIFF_V_DIM).transpose(0, 2, 1, 3)
    o = rms_norm(o, subln) * (1.0 - lam_init)
    return o.reshape(B, S, DIFF_V_WIDTH)


def rotary(t, cos, sin):
    half = t.shape[-1] // 2
    t1, t2 = t[..., :half], t[..., half:]
    return jnp.concatenate([t1 * cos - t2 * sin, t1 * sin + t2 * cos], axis=-1)


def retention(q, k, v, g):
    B, S, _ = q.shape
    H, DK, DV, C = N_RET_HEADS, RET_KEY_DIM, RET_V_DIM, RET_CHUNK
    q = q.reshape(B, S, H, DK).transpose(0, 2, 1, 3).astype(jnp.float32)
    k = k.reshape(B, S, H, DK).transpose(0, 2, 1, 3).astype(jnp.float32)
    v = v.reshape(B, S, H, DV).transpose(0, 2, 1, 3).astype(jnp.float32)
    pos = jnp.arange(S, dtype=jnp.float32)
    inv_freq = 1.0 / (ROPE_BASE ** jnp.linspace(0.0, 1.0, DK // 2, dtype=jnp.float32))
    ang = pos[:, None] * inv_freq[None, :]
    cos, sin = jnp.cos(ang), jnp.sin(ang)
    q = rotary(q, cos, sin)
    k = rotary(k, cos, sin) * (DK ** -0.5)
    log_g = jnp.log(1.0 - 2.0 ** (-5.0 - jnp.arange(H, dtype=jnp.float32)))
    idx = jnp.arange(C, dtype=jnp.float32)
    rel = idx[:, None] - idx[None, :]
    inner_decay = jnp.where(rel[None] >= 0,
                            jnp.exp(jnp.maximum(rel, 0.0)[None] * log_g[:, None, None]), 0.0)
    q_decay = jnp.exp((idx + 1.0)[None, :] * log_g[:, None])[..., None]
    k_decay = jnp.exp((C - 1.0 - idx)[None, :] * log_g[:, None])[..., None]
    chunk_decay = jnp.exp(C * log_g)[:, None, None]
    n_chunk = S // C

    def to_chunks(t):
        return jnp.moveaxis(t.reshape(B, H, n_chunk, C, t.shape[-1]), 2, 0)

    def step(state, qkv):
        qc, kc, vc = qkv
        scores = jnp.einsum('bhid,bhjd->bhij', qc, kc) * inner_decay
        out = (jnp.einsum('bhij,bhje->bhie', scores, vc)
               + jnp.einsum('bhid,bhde->bhie', qc * q_decay, state))
        state = state * chunk_decay + jnp.einsum('bhjd,bhje->bhde', kc * k_decay, vc)
        return state, out

    state0 = jnp.zeros((B, H, DK, DV), jnp.float32)
    _, o = lax.scan(step, state0, (to_chunks(q), to_chunks(k), to_chunks(v)))
    o = jnp.moveaxis(o, 0, 2).reshape(B, H, S, DV)
    o = o * lax.rsqrt(jnp.mean(o * o, axis=-1, keepdims=True) + EPS)
    o = o.transpose(0, 2, 1, 3).reshape(B, S, RET_V_WIDTH)
    return o * jax.nn.silu(g.astype(jnp.float32))


def parallel_mixer(h, w_in, w_out, rel_bias, lq1, lk1, lq2, lk2, subln, layer_idx):
    proj = jnp.einsum('bsd,de->bse', h, w_in)
    offs = np.cumsum(IN_SPLITS)[:-1].tolist()
    aq, ak, av, rq, rk, rv, rg = jnp.split(proj, offs, axis=-1)
    lam_init = 0.8 - 0.6 * math.exp(-0.3 * layer_idx)
    lam = (jnp.exp(jnp.sum(lq1.astype(jnp.float32) * lk1.astype(jnp.float32)))
           - jnp.exp(jnp.sum(lq2.astype(jnp.float32) * lk2.astype(jnp.float32))) + lam_init)
    a_out = diff_attention(aq, ak, av, rel_bias, lam, subln, lam_init)
    r_out = retention(rq, rk, rv, rg)
    mixed = jnp.concatenate([a_out, r_out], axis=-1).astype(h.dtype)
    return jnp.einsum('bse,ed->bsd', mixed, w_out)


def pool_mixer(h, w_pool, scale):
    B, S, D = h.shape
    hg = h.astype(jnp.float32).reshape(B, S, N_POOL_GROUPS, POOL_GROUP_DIM)
    cs = jnp.cumsum(hg, axis=1)
    t = jnp.arange(1, S + 1, dtype=jnp.float32)
    groups = []
    for gi, w in enumerate(POOL_WINDOWS):
        c = cs[:, :, gi]
        lagged = jnp.pad(c, ((0, 0), (w, 0), (0, 0)))[:, :S]
        count = jnp.minimum(t, float(w))[None, :, None]
        groups.append((c - lagged) / count - hg[:, :, gi])
    pooled = jnp.stack(groups, axis=2).astype(h.dtype)
    y = jnp.einsum('bsgc,gce->bsge', pooled, w_pool).reshape(B, S, D)
    return y * scale


def conv_ffn(h, w_up, conv_w, conv_b, w_down):
    u = jnp.einsum('bsd,df->bsf', h, w_up)
    S = u.shape[1]
    y = conv_b
    for tap in range(CONV_WIDTH):
        shifted = jnp.pad(u, ((0, 0), (tap, 0), (0, 0)))[:, :S]
        y = y + shifted * conv_w[CONV_WIDTH - 1 - tap]
    val, gate = jnp.split(y, 2, axis=-1)
    return jnp.einsum('bsf,fd->bsd', jax.nn.silu(gate) * val, w_down)


def setup_inputs(seed: int = 0) -> dict:
    key = jax.random.key(seed)
    ks = jax.random.split(key, 20)
    f32 = jnp.float32
    nrm = lambda k, shape, s: jax.random.normal(k, shape, f32) * s
    return {
        "x": nrm(ks[0], (BATCH, SEQ, D_MODEL), 1.0),
        "rel_bias": nrm(ks[1], (NUM_BUCKETS, N_DIFF_MAPS), 0.5),
        "mix_norm": 1.0 + nrm(ks[2], (DEPTH, D_MODEL), 0.02),
        "ffn_norm": 1.0 + nrm(ks[3], (DEPTH, D_MODEL), 0.02),
        "final_norm": 1.0 + nrm(ks[4], (D_MODEL,), 0.02),
        "w_in": nrm(ks[5], (N_PAR_LAYERS, D_MODEL, IN_WIDTH), D_MODEL ** -0.5),
        "w_out": nrm(ks[6], (N_PAR_LAYERS, MIX_WIDTH, D_MODEL), MIX_WIDTH ** -0.5),
        "lam_q1": nrm(ks[7], (N_PAR_LAYERS, DIFF_HEAD_DIM), 0.1),
        "lam_k1": nrm(ks[8], (N_PAR_LAYERS, DIFF_HEAD_DIM), 0.1),
        "lam_q2": nrm(ks[9], (N_PAR_LAYERS, DIFF_HEAD_DIM), 0.1),
        "lam_k2": nrm(ks[10], (N_PAR_LAYERS, DIFF_HEAD_DIM), 0.1),
        "diff_subln": 1.0 + nrm(ks[11], (N_PAR_LAYERS, DIFF_V_DIM), 0.02),
        "pool_w": nrm(ks[12], (N_POOL_LAYERS, N_POOL_GROUPS, POOL_GROUP_DIM, POOL_GROUP_DIM), POOL_GROUP_DIM ** -0.5),
        "pool_scale": 1.0 + nrm(ks[13], (N_POOL_LAYERS, D_MODEL), 0.02),
        "w_up": nrm(ks[14], (DEPTH, D_MODEL, 2 * D_FF), D_MODEL ** -0.5),
        "conv_w": nrm(ks[15], (DEPTH, CONV_WIDTH, 2 * D_FF), CONV_WIDTH ** -0.5),
        "conv_b": nrm(ks[16], (DEPTH, 2 * D_FF), 0.01),
        "w_down": nrm(ks[17], (DEPTH, D_FF, D_MODEL), D_FF ** -0.5),
    }


def reference(x, rel_bias, mix_norm, ffn_norm, final_norm, w_in, w_out, lam_q1, lam_k1,
              lam_q2, lam_k2, diff_subln, pool_w, pool_scale, w_up, conv_w, conv_b, w_down):
    for l in range(DEPTH):
        h = rms_norm(x, mix_norm[l])
        if l % 2 == 0:
            i = l // 2
            x = x + parallel_mixer(h, w_in[i], w_out[i], rel_bias, lam_q1[i], lam_k1[i],
                                   lam_q2[i], lam_k2[i], diff_subln[i], l).astype(x.dtype)
        else:
            j = l // 2
            x = x + pool_mixer(h, pool_w[j], pool_scale[j]).astype(x.dtype)
        h = rms_norm(x, ffn_norm[l])
        x = x + conv_ffn(h, w_up[l], conv_w[l], conv_b[l], w_down[l]).astype(x.dtype)
    return rms_norm(x, final_norm)
```

```python
import functools
import math

import jax
import jax.numpy as jnp
from jax import lax
from jax.experimental import pallas as pl
from jax.experimental.pallas import tpu as pltpu

F32 = jnp.float32
BF16 = jnp.bfloat16

N_DIFF_HEADS = 4
DIFF_HEAD_DIM = 128
DIFF_V_DIM = 2 * DIFF_HEAD_DIM
N_DIFF_MAPS = 2 * N_DIFF_HEADS
N_RET_HEADS = 4
RET_KEY_DIM = 256
RET_V_DIM = 256
ROPE_BASE = 10000.0
NUM_BUCKETS = 32
MAX_DISTANCE = 128
POOL_WINDOWS = (2, 4, 8, 16)
POOL_HALO = 16
CONV_WIDTH = 3
CONV_HALO = 8
EPS = 1e-6
NEG_INF = -1e30
PROJ_BLOCK = 1024
(BLK_AQ, BLK_AK, BLK_AV, BLK_RQ, BLK_RK, BLK_RV, BLK_RG) = range(7)

LANES = 128
VMEM_LIMIT = 56 * 1024 * 1024


def _params(*semantics):
    return pltpu.CompilerParams(dimension_semantics=semantics, vmem_limit_bytes=VMEM_LIMIT)


def _rms_norm(xf, g):
    y = xf * lax.rsqrt(jnp.mean(xf * xf, axis=-1, keepdims=True) + EPS)
    return y * g


def _silu(z):
    return z / (1.0 + jnp.exp(-z))


def _in_proj_kernel(x_ref, g_ref, w_ref, cos_ref, sin_ref, proj_ref, gate_ref, h_ref):
    j = pl.program_id(1)

    @pl.when(j == 0)
    def _():
        h_ref[...] = _rms_norm(x_ref[...], g_ref[...]).astype(BF16)

    acc = jnp.dot(h_ref[...], w_ref[...], preferred_element_type=F32)

    def rotary(scale):
        cos = cos_ref[...]
        sin = sin_ref[...]
        half = RET_KEY_DIM // 2
        for h in range(N_RET_HEADS):
            lo = h * RET_KEY_DIM
            t1 = acc[:, lo:lo + half]
            t2 = acc[:, lo + half:lo + RET_KEY_DIM]
            proj_ref[:, lo:lo + half] = ((t1 * cos - t2 * sin) * scale).astype(BF16)
            proj_ref[:, lo + half:lo + RET_KEY_DIM] = ((t1 * sin + t2 * cos) * scale).astype(BF16)

    @pl.when(j == BLK_AQ)
    def _():
        proj_ref[...] = (acc * (DIFF_HEAD_DIM ** -0.5)).astype(BF16)

    @pl.when((j == BLK_AK) | (j == BLK_AV) | (j == BLK_RV))
    def _():
        proj_ref[...] = acc.astype(BF16)

    @pl.when(j == BLK_RQ)
    def _():
        rotary(1.0)

    @pl.when(j == BLK_RK)
    def _():
        rotary(RET_KEY_DIM ** -0.5)

    @pl.when(j == BLK_RG)
    def _():
        gate_ref[...] = acc


def _in_proj(x, g, w, cos, sin, *, seq, tm):
    t, d = x.shape
    n_blk = w.shape[1] // PROJ_BLOCK
    assert n_blk == 7 and t % tm == 0 and seq % tm == 0
    tiles_per_seq = seq // tm
    return pl.pallas_call(
        _in_proj_kernel,
        grid=(t // tm, n_blk),
        in_specs=[
            pl.BlockSpec((tm, d), lambda i, j: (i, 0)),
            pl.BlockSpec((1, d), lambda i, j: (0, 0)),
            pl.BlockSpec((d, PROJ_BLOCK), lambda i, j: (0, j)),
            pl.BlockSpec((tm, RET_KEY_DIM // 2), lambda i, j: (i % tiles_per_seq, 0)),
            pl.BlockSpec((tm, RET_KEY_DIM // 2), lambda i, j: (i % tiles_per_seq, 0)),
        ],
        out_specs=[
            pl.BlockSpec((tm, PROJ_BLOCK), lambda i, j: (i, jnp.minimum(j, BLK_RV))),
            pl.BlockSpec((tm, PROJ_BLOCK), lambda i, j: (i, 0)),
        ],
        out_shape=[
            jax.ShapeDtypeStruct((t, BLK_RG * PROJ_BLOCK), BF16),
            jax.ShapeDtypeStruct((t, PROJ_BLOCK), F32),
        ],
        scratch_shapes=[pltpu.VMEM((tm, d), BF16)],
        compiler_params=_params("arbitrary", "arbitrary"),
        name="in_proj",
    )(x, g, w, cos, sin)


def _t5_bucket(dist):
    max_exact = NUM_BUCKETS // 2
    is_small = dist < max_exact
    df = jnp.maximum(dist, 1).astype(F32)
    large = max_exact + (jnp.log(df / max_exact) / math.log(MAX_DISTANCE / max_exact)
                         * (NUM_BUCKETS - max_exact)).astype(jnp.int32)
    large = jnp.minimum(large, NUM_BUCKETS - 1)
    return jnp.where(is_small, dist, large)


def _attn_kernel(qi_ref, ki_ref, q_ref, k_ref, v_ref, bias_ref, lam_ref, subln_ref, o_ref,
                 acc_ref, m_ref, l_ref, band_ref, *, ta, lam_init):
    b = pl.program_id(0)
    p = pl.program_id(1)
    qi = qi_ref[p]
    ki = ki_ref[p]
    n_sub = ta // LANES

    @pl.when((b == 0) & (p == 0))
    def _():
        row = lax.broadcasted_iota(jnp.int32, (LANES, LANES), 0)
        col = lax.broadcasted_iota(jnp.int32, (LANES, LANES), 1)
        for slot in range(2):
            bucket = _t5_bucket(jnp.maximum(row - col + slot * LANES, 0))
            for m in range(N_DIFF_MAPS):
                val = jnp.zeros((LANES, LANES), F32)
                for bk in range(NUM_BUCKETS):
                    val = jnp.where(bucket == bk, bias_ref[bk, m], val)
                band_ref[slot, m] = val - bias_ref[NUM_BUCKETS - 1, m]

    @pl.when(ki == 0)
    def _():
        m_ref[...] = jnp.full(m_ref.shape, NEG_INF, F32)
        l_ref[...] = jnp.zeros(l_ref.shape, F32)
        acc_ref[...] = jnp.zeros(acc_ref.shape, F32)

    def biased_slabs(s, m, kind):
        slabs = []
        for c in range(n_sub):
            sc = s[:, c * LANES:(c + 1) * LANES]
            if kind == "far" or (kind == "prev" and c != n_sub - 1):
                slabs.append(sc)
                continue
            pieces = []
            for a in range(n_sub):
                blk = sc[a * LANES:(a + 1) * LANES]
                if kind == "prev":
                    if a == 0:
                        blk = blk + band_ref[1, m]
                elif a < c:
                    blk = jnp.full((LANES, LANES), NEG_INF, F32)
                elif a == c:
                    row = lax.broadcasted_iota(jnp.int32, (LANES, LANES), 0)
                    col = lax.broadcasted_iota(jnp.int32, (LANES, LANES), 1)
                    blk = jnp.where(row >= col, blk + band_ref[0, m], NEG_INF)
                elif a == c + 1:
                    blk = blk + band_ref[1, m]
                pieces.append(blk)
            slabs.append(jnp.concatenate(pieces, axis=0))
        return slabs

    def step(kind):
        for h in range(N_DIFF_HEADS):
            v_h = v_ref[:, h * DIFF_V_DIM:(h + 1) * DIFF_V_DIM]
            for i in range(2):
                m = 2 * h + i
                q_m = q_ref[:, m * DIFF_HEAD_DIM:(m + 1) * DIFF_HEAD_DIM]
                k_m = k_ref[:, m * DIFF_HEAD_DIM:(m + 1) * DIFF_HEAD_DIM]
                s = lax.dot_general(q_m, k_m, (((1,), (1,)), ((), ())), preferred_element_type=F32)
                slabs = biased_slabs(s, m, kind)
                part_max = slabs[0]
                for sc in slabs[1:]:
                    part_max = jnp.maximum(part_max, sc)
                m_prev = m_ref[m]
                m_new = jnp.maximum(m_prev, jnp.max(part_max, axis=1, keepdims=True))
                alpha = jnp.exp(m_prev - m_new)
                probs = [jnp.exp(sc - m_new) for sc in slabs]
                part_sum = probs[0]
                for pc in probs[1:]:
                    part_sum = part_sum + pc
                m_ref[m] = m_new
                l_ref[m] = alpha * l_ref[m] + part_sum
                pv = jnp.dot(jnp.concatenate([pc.astype(BF16) for pc in probs], axis=1), v_h,
                             preferred_element_type=F32)
                alpha_v = jnp.concatenate([alpha] * (DIFF_V_DIM // LANES), axis=1)
                acc_ref[m] = acc_ref[m] * alpha_v + pv

    @pl.when(ki < qi - 1)
    def _():
        step("far")

    @pl.when((ki == qi - 1))
    def _():
        step("prev")

    @pl.when(ki == qi)
    def _():
        step("diag")
        lam_p = lam_ref[...]
        lam = (jnp.exp(jnp.sum(lam_p[0:1] * lam_p[1:2], axis=1, keepdims=True))
               - jnp.exp(jnp.sum(lam_p[2:3] * lam_p[3:4], axis=1, keepdims=True)) + lam_init)
        for h in range(N_DIFF_HEADS):
            inv0 = 1.0 / jnp.sum(l_ref[2 * h], axis=1, keepdims=True)
            inv1 = 1.0 / jnp.sum(l_ref[2 * h + 1], axis=1, keepdims=True)
            o = acc_ref[2 * h] * inv0 - lam * (acc_ref[2 * h + 1] * inv1)
            o = _rms_norm(o, subln_ref[...]) * (1.0 - lam_init)
            o_ref[:, h * DIFF_V_DIM:(h + 1) * DIFF_V_DIM] = o.astype(BF16)


def _diff_attention(proj, rel_bias, lam_params, subln, *, batch, seq, ta, lam_init):
    t = proj.shape[0]
    assert seq % ta == 0 and ta % LANES == 0 and ta >= 2 * LANES
    nq = seq // ta
    pairs = [(q, k) for q in range(nq) for k in range(q + 1)]
    qi = jnp.asarray([q for q, _ in pairs], jnp.int32)
    ki = jnp.asarray([k for _, k in pairs], jnp.int32)
    width = N_DIFF_HEADS * DIFF_V_DIM
    grid_spec = pltpu.PrefetchScalarGridSpec(
        num_scalar_prefetch=2,
        grid=(batch, len(pairs)),
        in_specs=[
            pl.BlockSpec((ta, PROJ_BLOCK), lambda b, p, qi, ki: (b * nq + qi[p], BLK_AQ)),
            pl.BlockSpec((ta, PROJ_BLOCK), lambda b, p, qi, ki: (b * nq + ki[p], BLK_AK)),
            pl.BlockSpec((ta, PROJ_BLOCK), lambda b, p, qi, ki: (b * nq + ki[p], BLK_AV)),
            pl.BlockSpec(memory_space=pltpu.SMEM),
            pl.BlockSpec((4, DIFF_HEAD_DIM), lambda b, p, qi, ki: (0, 0)),
            pl.BlockSpec((1, DIFF_V_DIM), lambda b, p, qi, ki: (0, 0)),
        ],
        out_specs=pl.BlockSpec((ta, width), lambda b, p, qi, ki: (b * nq + qi[p], 0)),
        scratch_shapes=[
            pltpu.VMEM((N_DIFF_MAPS, ta, DIFF_V_DIM), F32),
            pltpu.VMEM((N_DIFF_MAPS, ta, LANES), F32),
            pltpu.VMEM((N_DIFF_MAPS, ta, LANES), F32),
            pltpu.VMEM((2, N_DIFF_MAPS, LANES, LANES), F32),
        ],
    )
    return pl.pallas_call(
        functools.partial(_attn_kernel, ta=ta, lam_init=lam_init),
        grid_spec=grid_spec,
        out_shape=jax.ShapeDtypeStruct((t, width), BF16),
        compiler_params=_params("arbitrary", "arbitrary"),
        name="diff_attention",
    )(qi, ki, proj, proj, proj, rel_bias, lam_params, subln)


def _ret_log_decay(h):
    return math.log(1.0 - 2.0 ** (-5.0 - h))


def _retention_kernel(q_ref, k_ref, v_ref, g_ref, o_ref, state_ref, mask_ref, qdec_ref, kdec_ref, *, chunk):
    b = pl.program_id(0)
    c = pl.program_id(1)

    @pl.when((b == 0) & (c == 0))
    def _():
        rel = (lax.broadcasted_iota(jnp.int32, (chunk, chunk), 0)
               - lax.broadcasted_iota(jnp.int32, (chunk, chunk), 1)).astype(F32)
        idx = lax.broadcasted_iota(jnp.int32, (chunk, RET_KEY_DIM), 0).astype(F32)
        for h in range(N_RET_HEADS):
            log_g = _ret_log_decay(h)
            mask_ref[h] = jnp.where(rel >= 0, jnp.exp(jnp.maximum(rel, 0.0) * log_g), 0.0)
            qdec_ref[h] = jnp.exp((idx + 1.0) * log_g)
            kdec_ref[h] = jnp.exp((chunk - 1.0 - idx) * log_g)

    @pl.when(c == 0)
    def _():
        state_ref[...] = jnp.zeros(state_ref.shape, F32)

    for h in range(N_RET_HEADS):
        sl = slice(h * RET_KEY_DIM, (h + 1) * RET_KEY_DIM)
        q = q_ref[:, sl]
        k = k_ref[:, sl]
        v = v_ref[:, sl]
        scores = lax.dot_general(q, k, (((1,), (1,)), ((), ())), preferred_element_type=F32)
        inner = jnp.dot((scores * mask_ref[h]).astype(BF16), v, preferred_element_type=F32)
        state = state_ref[h]
        cross = jnp.dot(q, state.astype(BF16), preferred_element_type=F32) * qdec_ref[h]
        k_dec = (k.astype(F32) * kdec_ref[h]).astype(BF16)
        state_ref[h] = (state * math.exp(chunk * _ret_log_decay(h))
                        + lax.dot_general(k_dec, v, (((0,), (0,)), ((), ())), preferred_element_type=F32))
        o = inner + cross
        o = o * lax.rsqrt(jnp.mean(o * o, axis=-1, keepdims=True) + EPS)
        o_ref[:, sl] = (o * _silu(g_ref[:, sl])).astype(BF16)


def _retention(proj, gate, *, batch, seq, chunk):
    t = proj.shape[0]
    assert seq % chunk == 0
    nc = seq // chunk
    width = N_RET_HEADS * RET_V_DIM
    return pl.pallas_call(
        functools.partial(_retention_kernel, chunk=chunk),
        grid=(batch, nc),
        in_specs=[
            pl.BlockSpec((chunk, PROJ_BLOCK), lambda b, c: (b * nc + c, BLK_RQ)),
            pl.BlockSpec((chunk, PROJ_BLOCK), lambda b, c: (b * nc + c, BLK_RK)),
            pl.BlockSpec((chunk, PROJ_BLOCK), lambda b, c: (b * nc + c, BLK_RV)),
            pl.BlockSpec((chunk, PROJ_BLOCK), lambda b, c: (b * nc + c, 0)),
        ],
        out_specs=pl.BlockSpec((chunk, width), lambda b, c: (b * nc + c, 0)),
        out_shape=jax.ShapeDtypeStruct((t, width), BF16),
        scratch_shapes=[
            pltpu.VMEM((N_RET_HEADS, RET_KEY_DIM, RET_V_DIM), F32),
            pltpu.VMEM((N_RET_HEADS, chunk, chunk), F32),
            pltpu.VMEM((N_RET_HEADS, chunk, RET_KEY_DIM), F32),
            pltpu.VMEM((N_RET_HEADS, chunk, RET_KEY_DIM), F32),
        ],
        compiler_params=_params("arbitrary", "arbitrary"),
        name="retention",
    )(proj, proj, proj, gate)


def _out_proj_kernel(x_ref, a_ref, r_ref, w_ref, o_ref):
    ka = a_ref.shape[1]
    y = jnp.dot(a_ref[...], w_ref[:ka, :], preferred_element_type=F32)
    y = y + jnp.dot(r_ref[...], w_ref[ka:, :], preferred_element_type=F32)
    o_ref[...] = x_ref[...] + y


def _out_proj(x, a, r, w, *, tm, tn):
    t, d = x.shape
    ka, kr = a.shape[1], r.shape[1]
    assert t % tm == 0 and d % tn == 0 and w.shape == (ka + kr, d)
    return pl.pallas_call(
        _out_proj_kernel,
        grid=(t // tm, d // tn),
        in_specs=[
            pl.BlockSpec((tm, tn), lambda i, j: (i, j)),
            pl.BlockSpec((tm, ka), lambda i, j: (i, 0)),
            pl.BlockSpec((tm, kr), lambda i, j: (i, 0)),
            pl.BlockSpec((ka + kr, tn), lambda i, j: (0, j)),
        ],
        out_specs=pl.BlockSpec((tm, tn), lambda i, j: (i, j)),
        out_shape=jax.ShapeDtypeStruct((t, d), F32),
        compiler_params=_params("arbitrary", "arbitrary"),
        name="out_proj",
    )(x, a, r, w)


def _pool_kernel(x_ref, xp_ref, g_ref, w_ref, scale_ref, o_ref, h_ref, *, tm, seq):
    i = pl.program_id(0)
    start = (i * tm) % seq
    g = g_ref[...]
    h_prev = _rms_norm(xp_ref[...], g)
    h_ref[0:POOL_HALO, :] = jnp.where(start == 0, 0.0, h_prev)
    h_ref[POOL_HALO:, :] = _rms_norm(x_ref[...], g)
    pos = start + lax.broadcasted_iota(jnp.int32, (tm, 1), 0)
    gd = w_ref.shape[1]
    for gi, win in enumerate(POOL_WINDOWS):
        cols = slice(gi * gd, (gi + 1) * gd)
        hg = h_ref[POOL_HALO:, cols]
        total = hg
        for lag in range(1, win):
            total = total + h_ref[pl.ds(POOL_HALO - lag, tm), cols]
        inv_count = 1.0 / jnp.minimum(pos + 1, win).astype(F32)
        pooled = total * inv_count - hg
        y = jnp.dot(pooled.astype(BF16), w_ref[gi], preferred_element_type=F32)
        o_ref[:, cols] = x_ref[:, cols] + y * scale_ref[:, cols]


def _pool_mixer(x, g, w, scale, *, seq, tm):
    t, d = x.shape
    assert t % tm == 0 and seq % tm == 0 and tm % POOL_HALO == 0 and max(POOL_WINDOWS) <= POOL_HALO
    halo_blocks = tm // POOL_HALO
    return pl.pallas_call(
        functools.partial(_pool_kernel, tm=tm, seq=seq),
        grid=(t // tm,),
        in_specs=[
            pl.BlockSpec((tm, d), lambda i: (i, 0)),
            pl.BlockSpec((POOL_HALO, d), lambda i: (jnp.maximum(i * halo_blocks - 1, 0), 0)),
            pl.BlockSpec((1, d), lambda i: (0, 0)),
            pl.BlockSpec(w.shape, lambda i: (0, 0, 0)),
            pl.BlockSpec((1, d), lambda i: (0, 0)),
        ],
        out_specs=pl.BlockSpec((tm, d), lambda i: (i, 0)),
        out_shape=jax.ShapeDtypeStruct((t, d), F32),
        scratch_shapes=[pltpu.VMEM((tm + POOL_HALO, d), F32)],
        compiler_params=_params("arbitrary"),
        name="pool_mixer",
    )(x, x, g, w, scale)


def _ffn_kernel(*refs, tm, seq, final):
    if final:
        (x_ref, g_ref, wv_ref, wg_ref, cwv_ref, cwg_ref, cbv_ref, cbg_ref, wd_ref, gf_ref,
         o_ref, h_ref, acc_ref, tail_v_ref, tail_g_ref) = refs
    else:
        (x_ref, g_ref, wv_ref, wg_ref, cwv_ref, cwg_ref, cbv_ref, cbg_ref, wd_ref,
         o_ref, h_ref, acc_ref, tail_v_ref, tail_g_ref) = refs
    i = pl.program_id(0)
    f = pl.program_id(1)
    n_f = pl.num_programs(1)

    @pl.when(f == 0)
    def _():
        h_ref[...] = _rms_norm(x_ref[...], g_ref[...]).astype(BF16)

    @pl.when((i * tm) % seq == 0)
    def _():
        tail_v_ref[f] = jnp.zeros(tail_v_ref.shape[1:], F32)
        tail_g_ref[f] = jnp.zeros(tail_g_ref.shape[1:], F32)

    def conv(w_ref, cw_ref, cb_ref, tail_ref):
        u = jnp.dot(h_ref[...], w_ref[...], preferred_element_type=F32)
        ext = jnp.concatenate([tail_ref[f], u], axis=0)
        tail_ref[f] = u[tm - CONV_HALO:]
        y = cb_ref[...] + u * cw_ref[CONV_WIDTH - 1:CONV_WIDTH, :]
        for tap in range(1, CONV_WIDTH):
            shifted = pltpu.roll(ext, tap, 0)[CONV_HALO:]
            y = y + shifted * cw_ref[CONV_WIDTH - 1 - tap:CONV_WIDTH - tap, :]
        return y

    val = conv(wv_ref, cwv_ref, cbv_ref, tail_v_ref)
    gate = conv(wg_ref, cwg_ref, cbg_ref, tail_g_ref)
    act = (_silu(gate) * val).astype(BF16)
    down = jnp.dot(act, wd_ref[...], preferred_element_type=F32)

    @pl.when(f == 0)
    def _():
        acc_ref[...] = down

    @pl.when((f > 0) & (f < n_f - 1))
    def _():
        acc_ref[...] += down

    @pl.when(f == n_f - 1)
    def _():
        y = x_ref[...] + (acc_ref[...] + down)
        if final:
            y = _rms_norm(y, gf_ref[...])
        o_ref[...] = y


def _conv_ffn(x, g, w_up, conv_w, conv_b, w_down, final_g, *, seq, tm, tf):
    t, d = x.shape
    d_ff = w_down.shape[0]
    assert t % tm == 0 and seq % tm == 0 and d_ff % tf == 0 and w_up.shape == (d, 2 * d_ff)
    n_f = d_ff // tf
    assert n_f >= 2
    final = final_g is not None
    in_specs = [
        pl.BlockSpec((tm, d), lambda i, f: (i, 0)),
        pl.BlockSpec((1, d), lambda i, f: (0, 0)),
        pl.BlockSpec((d, tf), lambda i, f: (0, f)),
        pl.BlockSpec((d, tf), lambda i, f: (0, n_f + f)),
        pl.BlockSpec((CONV_WIDTH, tf), lambda i, f: (0, f)),
        pl.BlockSpec((CONV_WIDTH, tf), lambda i, f: (0, n_f + f)),
        pl.BlockSpec((1, tf), lambda i, f: (0, f)),
        pl.BlockSpec((1, tf), lambda i, f: (0, n_f + f)),
        pl.BlockSpec((tf, d), lambda i, f: (f, 0)),
    ]
    args = [x, g, w_up, w_up, conv_w, conv_w, conv_b, conv_b, w_down]
    if final:
        in_specs.append(pl.BlockSpec((1, d), lambda i, f: (0, 0)))
        args.append(final_g)
    return pl.pallas_call(
        functools.partial(_ffn_kernel, tm=tm, seq=seq, final=final),
        grid=(t // tm, n_f),
        in_specs=in_specs,
        out_specs=pl.BlockSpec((tm, d), lambda i, f: (i, 0)),
        out_shape=jax.ShapeDtypeStruct((t, d), F32),
        scratch_shapes=[
            pltpu.VMEM((tm, d), BF16),
            pltpu.VMEM((tm, d), F32),
            pltpu.VMEM((n_f, CONV_HALO, tf), F32),
            pltpu.VMEM((n_f, CONV_HALO, tf), F32),
        ],
        compiler_params=_params("arbitrary", "arbitrary"),
        name="conv_ffn",
    )(*args)


def _tiles(seq):
    pick = lambda want: min(want, seq)
    return dict(proj_tm=pick(1024), attn_ta=pick(512), ret_chunk=pick(512), out_tm=pick(1024),
                pool_tm=pick(512), ffn_tm=pick(512))


def kernel(x, rel_bias, mix_norm, ffn_norm, final_norm, w_in, w_out, lam_q1, lam_k1, lam_q2, lam_k2,
           diff_subln, pool_w, pool_scale, w_up, conv_w, conv_b, w_down):
    batch, seq, d = x.shape
    depth = mix_norm.shape[0]
    tiles = _tiles(seq)
    xf = x.reshape(batch * seq, d)

    pos = jnp.arange(seq, dtype=F32)
    inv_freq = 1.0 / (ROPE_BASE ** jnp.linspace(0.0, 1.0, RET_KEY_DIM // 2, dtype=F32))
    ang = pos[:, None] * inv_freq[None, :]
    cos, sin = jnp.cos(ang), jnp.sin(ang)

    for l in range(depth):
        g_mix = mix_norm[l][None, :]
        if l % 2 == 0:
            i = l // 2
            lam_init = 0.8 - 0.6 * math.exp(-0.3 * l)
            proj, gate = _in_proj(xf, g_mix, w_in[i].astype(BF16), cos, sin, seq=seq, tm=tiles["proj_tm"])
            lam_params = jnp.stack([lam_q1[i], lam_k1[i], lam_q2[i], lam_k2[i]]).astype(F32)
            a_out = _diff_attention(proj, rel_bias, lam_params, diff_subln[i][None, :], batch=batch, seq=seq,
                                    ta=tiles["attn_ta"], lam_init=lam_init)
            r_out = _retention(proj, gate, batch=batch, seq=seq, chunk=tiles["ret_chunk"])
            xf = _out_proj(xf, a_out, r_out, w_out[i].astype(BF16), tm=tiles["out_tm"], tn=PROJ_BLOCK)
        else:
            j = l // 2
            xf = _pool_mixer(xf, g_mix, pool_w[j].astype(BF16), pool_scale[j][None, :], seq=seq,
                             tm=tiles["pool_tm"])
        final_g = final_norm[None, :] if l == depth - 1 else None
        xf = _conv_ffn(xf, ffn_norm[l][None, :], w_up[l].astype(BF16), conv_w[l], conv_b[l][None, :],
                       w_down[l].astype(BF16), final_g, seq=seq, tm=tiles["ffn_tm"], tf=512)
    return xf.reshape(batch, seq, d)
```

```python
import functools
import math

import jax
import jax.numpy as jnp
from jax import lax
from jax.experimental import pallas as pl
from jax.experimental.pallas import tpu as pltpu

F32 = jnp.float32
BF16 = jnp.bfloat16

N_DIFF_HEADS = 4
DIFF_HEAD_DIM = 128
DIFF_V_DIM = 2 * DIFF_HEAD_DIM
N_DIFF_MAPS = 2 * N_DIFF_HEADS
N_RET_HEADS = 4
RET_KEY_DIM = 256
RET_V_DIM = 256
ROPE_BASE = 10000.0
NUM_BUCKETS = 32
MAX_DISTANCE = 128
POOL_WINDOWS = (2, 4, 8, 16)
POOL_HALO = 16
CONV_WIDTH = 3
CONV_HALO = 8
EPS = 1e-6
NEG_INF = -1e30
LOG2_E = math.log2(math.e)
PROJ_BLOCK = 1024
(BLK_AQ, BLK_AK, BLK_AV, BLK_RQ, BLK_RK, BLK_RV, BLK_RG) = range(7)

LANES = 128
VMEM_LIMIT = 56 * 1024 * 1024


def _params(*semantics):
    return pltpu.CompilerParams(dimension_semantics=semantics, vmem_limit_bytes=VMEM_LIMIT)


def _rms_norm(xf, g):
    y = xf * lax.rsqrt(jnp.mean(xf * xf, axis=-1, keepdims=True) + EPS)
    return y * g


def _silu(z):
    return z / (1.0 + jnp.exp(-z))


def _in_proj_kernel(x_ref, g_ref, w_ref, cos_ref, sin_ref, proj_ref, gate_ref, h_ref):
    j = pl.program_id(1)

    @pl.when(j == 0)
    def _():
        h_ref[...] = _rms_norm(x_ref[...], g_ref[...]).astype(BF16)

    acc = jnp.dot(h_ref[...], w_ref[...], preferred_element_type=F32)

    def rotary(scale):
        cos = cos_ref[...]
        sin = sin_ref[...]
        half = RET_KEY_DIM // 2
        for h in range(N_RET_HEADS):
            lo = h * RET_KEY_DIM
            t1 = acc[:, lo:lo + half]
            t2 = acc[:, lo + half:lo + RET_KEY_DIM]
            proj_ref[:, lo:lo + half] = ((t1 * cos - t2 * sin) * scale).astype(BF16)
            proj_ref[:, lo + half:lo + RET_KEY_DIM] = ((t1 * sin + t2 * cos) * scale).astype(BF16)

    @pl.when(j == BLK_AQ)
    def _():
        proj_ref[...] = (acc * (DIFF_HEAD_DIM ** -0.5 * LOG2_E)).astype(BF16)

    @pl.when((j == BLK_AK) | (j == BLK_AV) | (j == BLK_RV))
    def _():
        proj_ref[...] = acc.astype(BF16)

    @pl.when(j == BLK_RQ)
    def _():
        rotary(1.0)

    @pl.when(j == BLK_RK)
    def _():
        rotary(RET_KEY_DIM ** -0.5)

    @pl.when(j == BLK_RG)
    def _():
        gate_ref[...] = acc


def _in_proj(x, g, w, cos, sin, *, seq, tm):
    t, d = x.shape
    n_blk = w.shape[1] // PROJ_BLOCK
    assert n_blk == 7 and t % tm == 0 and seq % tm == 0
    tiles_per_seq = seq // tm
    return pl.pallas_call(
        _in_proj_kernel,
        grid=(t // tm, n_blk),
        in_specs=[
            pl.BlockSpec((tm, d), lambda i, j: (i, 0)),
            pl.BlockSpec((1, d), lambda i, j: (0, 0)),
            pl.BlockSpec((d, PROJ_BLOCK), lambda i, j: (0, j)),
            pl.BlockSpec((tm, RET_KEY_DIM // 2), lambda i, j: (i % tiles_per_seq, 0)),
            pl.BlockSpec((tm, RET_KEY_DIM // 2), lambda i, j: (i % tiles_per_seq, 0)),
        ],
        out_specs=[
            pl.BlockSpec((tm, PROJ_BLOCK), lambda i, j: (i, jnp.minimum(j, BLK_RV))),
            pl.BlockSpec((tm, PROJ_BLOCK), lambda i, j: (i, 0)),
        ],
        out_shape=[
            jax.ShapeDtypeStruct((t, BLK_RG * PROJ_BLOCK), BF16),
            jax.ShapeDtypeStruct((t, PROJ_BLOCK), F32),
        ],
        scratch_shapes=[pltpu.VMEM((tm, d), BF16)],
        compiler_params=_params("arbitrary", "arbitrary"),
        name="in_proj",
    )(x, g, w, cos, sin)


def _t5_bucket(dist):
    max_exact = NUM_BUCKETS // 2
    is_small = dist < max_exact
    df = jnp.maximum(dist, 1).astype(F32)
    large = max_exact + (jnp.log(df / max_exact) / math.log(MAX_DISTANCE / max_exact)
                         * (NUM_BUCKETS - max_exact)).astype(jnp.int32)
    large = jnp.minimum(large, NUM_BUCKETS - 1)
    return jnp.where(is_small, dist, large)


def _attn_kernel(qi_ref, ki_ref, q_ref, k_ref, v_ref, bias_ref, lam_ref, subln_ref, o_ref,
                 acc_ref, m_ref, l_ref, band_ref, *, ta, lam_init):
    b = pl.program_id(0)
    p = pl.program_id(1)
    qi = qi_ref[p]
    ki = ki_ref[p]
    n_sub = ta // LANES

    @pl.when((b == 0) & (p == 0))
    def _():
        row = lax.broadcasted_iota(jnp.int32, (LANES, LANES), 0)
        col = lax.broadcasted_iota(jnp.int32, (LANES, LANES), 1)
        for slot in range(2):
            bucket = _t5_bucket(jnp.maximum(row - col + slot * LANES, 0))
            for m in range(N_DIFF_MAPS):
                val = jnp.zeros((LANES, LANES), F32)
                for bk in range(NUM_BUCKETS):
                    val = jnp.where(bucket == bk, bias_ref[bk, m], val)
                band_ref[slot, m] = (val - bias_ref[NUM_BUCKETS - 1, m]) * LOG2_E

    @pl.when(ki == 0)
    def _():
        m_ref[...] = jnp.full(m_ref.shape, NEG_INF, F32)
        l_ref[...] = jnp.zeros(l_ref.shape, F32)
        acc_ref[...] = jnp.zeros(acc_ref.shape, F32)

    def biased_slabs(s, m, kind):
        slabs = []
        for c in range(n_sub):
            sc = s[:, c * LANES:(c + 1) * LANES]
            if kind == "far" or (kind == "prev" and c != n_sub - 1):
                slabs.append(sc)
                continue
            pieces = []
            for a in range(n_sub):
                blk = sc[a * LANES:(a + 1) * LANES]
                if kind == "prev":
                    if a == 0:
                        blk = blk + band_ref[1, m]
                elif a < c:
                    blk = jnp.full((LANES, LANES), NEG_INF, F32)
                elif a == c:
                    row = lax.broadcasted_iota(jnp.int32, (LANES, LANES), 0)
                    col = lax.broadcasted_iota(jnp.int32, (LANES, LANES), 1)
                    blk = jnp.where(row >= col, blk + band_ref[0, m], NEG_INF)
                elif a == c + 1:
                    blk = blk + band_ref[1, m]
                pieces.append(blk)
            slabs.append(jnp.concatenate(pieces, axis=0))
        return slabs

    def scores(m):
        q_m = q_ref[:, m * DIFF_HEAD_DIM:(m + 1) * DIFF_HEAD_DIM]
        k_m = k_ref[:, m * DIFF_HEAD_DIM:(m + 1) * DIFF_HEAD_DIM]
        return lax.dot_general(q_m, k_m, (((1,), (1,)), ((), ())), preferred_element_type=F32)

    def step(kind):
        for m in range(N_DIFF_MAPS):
            s = scores(m)
            h = m // 2
            v_h = v_ref[:, h * DIFF_V_DIM:(h + 1) * DIFF_V_DIM]
            slabs = biased_slabs(s, m, kind)
            part_max = slabs[0]
            for sc in slabs[1:]:
                part_max = jnp.maximum(part_max, sc)
            m_prev = m_ref[m]
            m_new = jnp.maximum(m_prev, jnp.max(part_max, axis=1, keepdims=True))
            alpha = jnp.exp2(m_prev - m_new)
            probs = [jnp.exp2(sc - m_new) for sc in slabs]
            part_sum = probs[0]
            for pc in probs[1:]:
                part_sum = part_sum + pc
            m_ref[m] = m_new
            l_ref[m] = alpha * l_ref[m] + part_sum
            pv = jnp.dot(jnp.concatenate([pc.astype(BF16) for pc in probs], axis=1), v_h,
                         preferred_element_type=F32)
            alpha_v = jnp.concatenate([alpha] * (DIFF_V_DIM // LANES), axis=1)
            acc_ref[m] = acc_ref[m] * alpha_v + pv

    @pl.when(ki < qi - 1)
    def _():
        step("far")

    @pl.when((ki == qi - 1))
    def _():
        step("prev")

    @pl.when(ki == qi)
    def _():
        step("diag")
        lam_p = lam_ref[...]
        lam = (jnp.exp(jnp.sum(lam_p[0:1] * lam_p[1:2], axis=1, keepdims=True))
               - jnp.exp(jnp.sum(lam_p[2:3] * lam_p[3:4], axis=1, keepdims=True)) + lam_init)
        for h in range(N_DIFF_HEADS):
            inv0 = 1.0 / jnp.sum(l_ref[2 * h], axis=1, keepdims=True)
            inv1 = 1.0 / jnp.sum(l_ref[2 * h + 1], axis=1, keepdims=True)
            o = acc_ref[2 * h] * inv0 - lam * (acc_ref[2 * h + 1] * inv1)
            o = _rms_norm(o, subln_ref[...]) * (1.0 - lam_init)
            o_ref[:, h * DIFF_V_DIM:(h + 1) * DIFF_V_DIM] = o.astype(BF16)


def _diff_attention(proj, rel_bias, lam_params, subln, *, batch, seq, ta, lam_init):
    t = proj.shape[0]
    assert seq % ta == 0 and ta % LANES == 0 and ta >= 2 * LANES
    nq = seq // ta
    pairs = [(q, k) for q in range(nq) for k in range(q + 1)]
    qi = jnp.asarray([q for q, _ in pairs], jnp.int32)
    ki = jnp.asarray([k for _, k in pairs], jnp.int32)
    width = N_DIFF_HEADS * DIFF_V_DIM
    grid_spec = pltpu.PrefetchScalarGridSpec(
        num_scalar_prefetch=2,
        grid=(batch, len(pairs)),
        in_specs=[
            pl.BlockSpec((ta, PROJ_BLOCK), lambda b, p, qi, ki: (b * nq + qi[p], BLK_AQ)),
            pl.BlockSpec((ta, PROJ_BLOCK), lambda b, p, qi, ki: (b * nq + ki[p], BLK_AK)),
            pl.BlockSpec((ta, PROJ_BLOCK), lambda b, p, qi, ki: (b * nq + ki[p], BLK_AV)),
            pl.BlockSpec(memory_space=pltpu.SMEM),
            pl.BlockSpec((4, DIFF_HEAD_DIM), lambda b, p, qi, ki: (0, 0)),
            pl.BlockSpec((1, DIFF_V_DIM), lambda b, p, qi, ki: (0, 0)),
        ],
        out_specs=pl.BlockSpec((ta, width), lambda b, p, qi, ki: (b * nq + qi[p], 0)),
        scratch_shapes=[
            pltpu.VMEM((N_DIFF_MAPS, ta, DIFF_V_DIM), F32),
            pltpu.VMEM((N_DIFF_MAPS, ta, LANES), F32),
            pltpu.VMEM((N_DIFF_MAPS, ta, LANES), F32),
            pltpu.VMEM((2, N_DIFF_MAPS, LANES, LANES), F32),
        ],
    )
    return pl.pallas_call(
        functools.partial(_attn_kernel, ta=ta, lam_init=lam_init),
        grid_spec=grid_spec,
        out_shape=jax.ShapeDtypeStruct((t, width), BF16),
        compiler_params=_params("arbitrary", "arbitrary"),
        name="diff_attention",
    )(qi, ki, proj, proj, proj, rel_bias, lam_params, subln)


def _ret_log_decay(h):
    return math.log(1.0 - 2.0 ** (-5.0 - h))


def _retention_kernel(q_ref, k_ref, v_ref, g_ref, o_ref, state_ref, mask_ref, qdec_ref, kdec_ref, *, chunk):
    b = pl.program_id(0)
    c = pl.program_id(1)

    @pl.when((b == 0) & (c == 0))
    def _():
        rel = (lax.broadcasted_iota(jnp.int32, (chunk, chunk), 0)
               - lax.broadcasted_iota(jnp.int32, (chunk, chunk), 1)).astype(F32)
        idx = lax.broadcasted_iota(jnp.int32, (chunk, RET_KEY_DIM), 0).astype(F32)
        for h in range(N_RET_HEADS):
            log_g = _ret_log_decay(h)
            mask_ref[h] = jnp.where(rel >= 0, jnp.exp(jnp.maximum(rel, 0.0) * log_g), 0.0)
            qdec_ref[h] = jnp.exp((idx + 1.0) * log_g)
            kdec_ref[h] = jnp.exp((chunk - 1.0 - idx) * log_g)

    @pl.when(c == 0)
    def _():
        state_ref[...] = jnp.zeros(state_ref.shape, F32)

    for h in range(N_RET_HEADS):
        sl = slice(h * RET_KEY_DIM, (h + 1) * RET_KEY_DIM)
        q = q_ref[:, sl]
        k = k_ref[:, sl]
        v = v_ref[:, sl]
        scores = lax.dot_general(q, k, (((1,), (1,)), ((), ())), preferred_element_type=F32)
        inner = jnp.dot((scores * mask_ref[h]).astype(BF16), v, preferred_element_type=F32)
        state = state_ref[h]
        cross = jnp.dot(q, state.astype(BF16), preferred_element_type=F32) * qdec_ref[h]
        k_dec = (k.astype(F32) * kdec_ref[h]).astype(BF16)
        state_ref[h] = (state * math.exp(chunk * _ret_log_decay(h))
                        + lax.dot_general(k_dec, v, (((0,), (0,)), ((), ())), preferred_element_type=F32))
        o = inner + cross
        o = o * lax.rsqrt(jnp.mean(o * o, axis=-1, keepdims=True) + EPS)
        o_ref[:, sl] = (o * _silu(g_ref[:, sl])).astype(BF16)


def _retention(proj, gate, *, batch, seq, chunk):
    t = proj.shape[0]
    assert seq % chunk == 0
    nc = seq // chunk
    width = N_RET_HEADS * RET_V_DIM
    return pl.pallas_call(
        functools.partial(_retention_kernel, chunk=chunk),
        grid=(batch, nc),
        in_specs=[
            pl.BlockSpec((chunk, PROJ_BLOCK), lambda b, c: (b * nc + c, BLK_RQ)),
            pl.BlockSpec((chunk, PROJ_BLOCK), lambda b, c: (b * nc + c, BLK_RK)),
            pl.BlockSpec((chunk, PROJ_BLOCK), lambda b, c: (b * nc + c, BLK_RV)),
            pl.BlockSpec((chunk, PROJ_BLOCK), lambda b, c: (b * nc + c, 0)),
        ],
        out_specs=pl.BlockSpec((chunk, width), lambda b, c: (b * nc + c, 0)),
        out_shape=jax.ShapeDtypeStruct((t, width), BF16),
        scratch_shapes=[
            pltpu.VMEM((N_RET_HEADS, RET_KEY_DIM, RET_V_DIM), F32),
            pltpu.VMEM((N_RET_HEADS, chunk, chunk), F32),
            pltpu.VMEM((N_RET_HEADS, chunk, RET_KEY_DIM), F32),
            pltpu.VMEM((N_RET_HEADS, chunk, RET_KEY_DIM), F32),
        ],
        compiler_params=_params("arbitrary", "arbitrary"),
        name="retention",
    )(proj, proj, proj, gate)


def _out_proj_kernel(x_ref, a_ref, r_ref, w_ref, o_ref):
    ka = a_ref.shape[1]
    y = jnp.dot(a_ref[...], w_ref[:ka, :], preferred_element_type=F32)
    y = y + jnp.dot(r_ref[...], w_ref[ka:, :], preferred_element_type=F32)
    o_ref[...] = x_ref[...] + y


def _out_proj(x, a, r, w, *, tm, tn):
    t, d = x.shape
    ka, kr = a.shape[1], r.shape[1]
    assert t % tm == 0 and d % tn == 0 and w.shape == (ka + kr, d)
    return pl.pallas_call(
        _out_proj_kernel,
        grid=(t // tm, d // tn),
        in_specs=[
            pl.BlockSpec((tm, tn), lambda i, j: (i, j)),
            pl.BlockSpec((tm, ka), lambda i, j: (i, 0)),
            pl.BlockSpec((tm, kr), lambda i, j: (i, 0)),
            pl.BlockSpec((ka + kr, tn), lambda i, j: (0, j)),
        ],
        out_specs=pl.BlockSpec((tm, tn), lambda i, j: (i, j)),
        out_shape=jax.ShapeDtypeStruct((t, d), F32),
        compiler_params=_params("arbitrary", "arbitrary"),
        name="out_proj",
    )(x, a, r, w)


def _pool_kernel(x_ref, xp_ref, g_ref, w_ref, scale_ref, o_ref, h_ref, *, tm, seq):
    i = pl.program_id(0)
    start = (i * tm) % seq
    g = g_ref[...]
    h_prev = _rms_norm(xp_ref[...], g)
    h_ref[0:POOL_HALO, :] = jnp.where(start == 0, 0.0, h_prev)
    h_ref[POOL_HALO:, :] = _rms_norm(x_ref[...], g)
    pos = start + lax.broadcasted_iota(jnp.int32, (tm, 1), 0)
    gd = w_ref.shape[1]
    for gi, win in enumerate(POOL_WINDOWS):
        cols = slice(gi * gd, (gi + 1) * gd)
        total = h_ref[:, cols]
        hg = total[POOL_HALO:]
        span = 1
        while span < win:
            total = total + pltpu.roll(total, span, 0)
            span *= 2
        inv_count = 1.0 / jnp.minimum(pos + 1, win).astype(F32)
        pooled = total[POOL_HALO:] * inv_count - hg
        y = jnp.dot(pooled.astype(BF16), w_ref[gi], preferred_element_type=F32)
        o_ref[:, cols] = x_ref[:, cols] + y * scale_ref[:, cols]


def _pool_mixer(x, g, w, scale, *, seq, tm):
    t, d = x.shape
    assert t % tm == 0 and seq % tm == 0 and tm % POOL_HALO == 0 and max(POOL_WINDOWS) <= POOL_HALO
    assert all(win & (win - 1) == 0 for win in POOL_WINDOWS)
    halo_blocks = tm // POOL_HALO
    return pl.pallas_call(
        functools.partial(_pool_kernel, tm=tm, seq=seq),
        grid=(t // tm,),
        in_specs=[
            pl.BlockSpec((tm, d), lambda i: (i, 0)),
            pl.BlockSpec((POOL_HALO, d), lambda i: (jnp.maximum(i * halo_blocks - 1, 0), 0)),
            pl.BlockSpec((1, d), lambda i: (0, 0)),
            pl.BlockSpec(w.shape, lambda i: (0, 0, 0)),
            pl.BlockSpec((1, d), lambda i: (0, 0)),
        ],
        out_specs=pl.BlockSpec((tm, d), lambda i: (i, 0)),
        out_shape=jax.ShapeDtypeStruct((t, d), F32),
        scratch_shapes=[pltpu.VMEM((tm + POOL_HALO, d), F32)],
        compiler_params=_params("arbitrary"),
        name="pool_mixer",
    )(x, x, g, w, scale)


def _ffn_kernel(*refs, tm, seq, final, n_sub):
    if final:
        (x_ref, g_ref, wv_ref, wg_ref, cwv_ref, cwg_ref, cbv_ref, cbg_ref, wd_ref, gf_ref,
         o_ref, h_ref, tail_v_ref, tail_g_ref) = refs
    else:
        (x_ref, g_ref, wv_ref, wg_ref, cwv_ref, cwg_ref, cbv_ref, cbg_ref, wd_ref,
         o_ref, h_ref, tail_v_ref, tail_g_ref) = refs
    i = pl.program_id(0)
    f = pl.program_id(1)
    n_f = pl.num_programs(1)
    tsub = tm // n_sub

    @pl.when(f == 0)
    def _():
        x = x_ref[...]
        h_ref[...] = _rms_norm(x, g_ref[...]).astype(BF16)
        o_ref[...] = x

    @pl.when((i * tm) % seq == 0)
    def _():
        tail_v_ref[f] = jnp.zeros(tail_v_ref.shape[1:], F32)
        tail_g_ref[f] = jnp.zeros(tail_g_ref.shape[1:], F32)

    def conv(u, prev, cw_ref, cb_ref):
        ext = jnp.concatenate([prev, u], axis=0)
        y = cb_ref[...] + u * cw_ref[CONV_WIDTH - 1:CONV_WIDTH, :]
        for tap in range(1, CONV_WIDTH):
            shifted = pltpu.roll(ext, tap, 0)[CONV_HALO:]
            y = y + shifted * cw_ref[CONV_WIDTH - 1 - tap:CONV_WIDTH - tap, :]
        return y

    prev_v = tail_v_ref[f]
    prev_g = tail_g_ref[f]
    def up(s):
        h_s = h_ref[s * tsub:(s + 1) * tsub, :]
        return (jnp.dot(h_s, wv_ref[...], preferred_element_type=F32),
                jnp.dot(h_s, wg_ref[...], preferred_element_type=F32))

    def down(s, u, prev):
        val = conv(u[0], prev[0], cwv_ref, cbv_ref)
        gate = conv(u[1], prev[1], cwg_ref, cbg_ref)
        act = (_silu(gate) * val).astype(BF16)
        o_ref[s * tsub:(s + 1) * tsub, :] += jnp.dot(act, wd_ref[...], preferred_element_type=F32)

    prev = (tail_v_ref[f], tail_g_ref[f])
    u_cur = up(0)
    for s in range(n_sub):
        u_next = up(s + 1) if s + 1 < n_sub else None
        down(s, u_cur, prev)
        prev = (u_cur[0][tsub - CONV_HALO:], u_cur[1][tsub - CONV_HALO:])
        u_cur = u_next
    tail_v_ref[f] = prev[0]
    tail_g_ref[f] = prev[1]

    if final:
        @pl.when(f == n_f - 1)
        def _():
            o_ref[...] = _rms_norm(o_ref[...], gf_ref[...])


def _conv_ffn(x, g, w_up, conv_w, conv_b, w_down, final_g, *, seq, tm, tf, n_sub):
    t, d = x.shape
    d_ff = w_down.shape[0]
    assert t % tm == 0 and seq % tm == 0 and d_ff % tf == 0 and w_up.shape == (d, 2 * d_ff)
    n_f = d_ff // tf
    assert tm % n_sub == 0 and (tm // n_sub) % 16 == 0
    final = final_g is not None
    in_specs = [
        pl.BlockSpec((tm, d), lambda i, f: (i, 0)),
        pl.BlockSpec((1, d), lambda i, f: (0, 0)),
        pl.BlockSpec((d, tf), lambda i, f: (0, f)),
        pl.BlockSpec((d, tf), lambda i, f: (0, n_f + f)),
        pl.BlockSpec((CONV_WIDTH, tf), lambda i, f: (0, f)),
        pl.BlockSpec((CONV_WIDTH, tf), lambda i, f: (0, n_f + f)),
        pl.BlockSpec((1, tf), lambda i, f: (0, f)),
        pl.BlockSpec((1, tf), lambda i, f: (0, n_f + f)),
        pl.BlockSpec((tf, d), lambda i, f: (f, 0)),
    ]
    args = [x, g, w_up, w_up, conv_w, conv_w, conv_b, conv_b, w_down]
    if final:
        in_specs.append(pl.BlockSpec((1, d), lambda i, f: (0, 0)))
        args.append(final_g)
    return pl.pallas_call(
        functools.partial(_ffn_kernel, tm=tm, seq=seq, final=final, n_sub=n_sub),
        grid=(t // tm, n_f),
        in_specs=in_specs,
        out_specs=pl.BlockSpec((tm, d), lambda i, f: (i, 0)),
        out_shape=jax.ShapeDtypeStruct((t, d), F32),
        scratch_shapes=[
            pltpu.VMEM((tm, d), BF16),
            pltpu.VMEM((n_f, CONV_HALO, tf), F32),
            pltpu.VMEM((n_f, CONV_HALO, tf), F32),
        ],
        compiler_params=_params("arbitrary", "arbitrary"),
        name="conv_ffn",
    )(*args)


def _tiles(seq):
    pick = lambda want: min(want, seq)
    return dict(proj_tm=pick(1024), attn_ta=pick(1024), ret_chunk=pick(512), out_tm=pick(1024),
                pool_tm=pick(512), ffn_tm=pick(1024))


def kernel(x, rel_bias, mix_norm, ffn_norm, final_norm, w_in, w_out, lam_q1, lam_k1, lam_q2, lam_k2,
           diff_subln, pool_w, pool_scale, w_up, conv_w, conv_b, w_down):
    batch, seq, d = x.shape
    depth = mix_norm.shape[0]
    tiles = _tiles(seq)
    xf = x.reshape(batch * seq, d)

    pos = jnp.arange(seq, dtype=F32)
    inv_freq = 1.0 / (ROPE_BASE ** jnp.linspace(0.0, 1.0, RET_KEY_DIM // 2, dtype=F32))
    ang = pos[:, None] * inv_freq[None, :]
    cos, sin = jnp.cos(ang), jnp.sin(ang)

    for l in range(depth):
        g_mix = mix_norm[l][None, :]
        if l % 2 == 0:
            i = l // 2
            lam_init = 0.8 - 0.6 * math.exp(-0.3 * l)
            proj, gate = _in_proj(xf, g_mix, w_in[i].astype(BF16), cos, sin, seq=seq, tm=tiles["proj_tm"])
            lam_params = jnp.stack([lam_q1[i], lam_k1[i], lam_q2[i], lam_k2[i]]).astype(F32)
            a_out = _diff_attention(proj, rel_bias, lam_params, diff_subln[i][None, :], batch=batch, seq=seq,
                                    ta=tiles["attn_ta"], lam_init=lam_init)
            r_out = _retention(proj, gate, batch=batch, seq=seq, chunk=tiles["ret_chunk"])
            xf = _out_proj(xf, a_out, r_out, w_out[i].astype(BF16), tm=tiles["out_tm"], tn=PROJ_BLOCK)
        else:
            j = l // 2
            xf = _pool_mixer(xf, g_mix, pool_w[j].astype(BF16), pool_scale[j][None, :], seq=seq,
                             tm=tiles["pool_tm"])
        final_g = final_norm[None, :] if l == depth - 1 else None
        xf = _conv_ffn(xf, ffn_norm[l][None, :], w_up[l].astype(BF16), conv_w[l], conv_b[l][None, :],
                       w_down[l].astype(BF16), final_g, seq=seq, tm=tiles["ffn_tm"], tf=512, n_sub=4)
    return xf.reshape(batch, seq, d)
```

```python
import functools
import math

import jax
import jax.numpy as jnp
from jax import lax
from jax.experimental import pallas as pl
from jax.experimental.pallas import tpu as pltpu

F32 = jnp.float32
BF16 = jnp.bfloat16

N_DIFF_HEADS = 4
DIFF_HEAD_DIM = 128
DIFF_V_DIM = 2 * DIFF_HEAD_DIM
N_DIFF_MAPS = 2 * N_DIFF_HEADS
N_RET_HEADS = 4
RET_KEY_DIM = 256
RET_V_DIM = 256
ROPE_BASE = 10000.0
NUM_BUCKETS = 32
MAX_DISTANCE = 128
POOL_WINDOWS = (2, 4, 8, 16)
POOL_HALO = 16
CONV_WIDTH = 3
CONV_HALO = 8
EPS = 1e-6
NEG_INF = -1e30
LOG2_E = math.log2(math.e)
PROJ_BLOCK = 1024
(BLK_AQ, BLK_AK, BLK_AV, BLK_RQ, BLK_RK, BLK_RV, BLK_RG) = range(7)

LANES = 128
VMEM_LIMIT = 56 * 1024 * 1024


def _params(*semantics):
    return pltpu.CompilerParams(dimension_semantics=semantics, vmem_limit_bytes=VMEM_LIMIT)


def _rms_norm(xf, g):
    y = xf * lax.rsqrt(jnp.mean(xf * xf, axis=-1, keepdims=True) + EPS)
    return y * g


def _silu(z):
    return z / (1.0 + jnp.exp(-z))


def _silu_tanh(z):
    half = 0.5 * z
    return half + half * jnp.tanh(half)


def _in_proj_kernel(x_ref, g_ref, w_ref, cos_ref, sin_ref, proj_ref, gate_ref, h_ref):
    j = pl.program_id(1)

    @pl.when(j == 0)
    def _():
        h_ref[...] = _rms_norm(x_ref[...], g_ref[...]).astype(BF16)

    acc = jnp.dot(h_ref[...], w_ref[...], preferred_element_type=F32)

    def rotary(scale):
        cos = cos_ref[...]
        sin = sin_ref[...]
        half = RET_KEY_DIM // 2
        for h in range(N_RET_HEADS):
            lo = h * RET_KEY_DIM
            t1 = acc[:, lo:lo + half]
            t2 = acc[:, lo + half:lo + RET_KEY_DIM]
            proj_ref[:, lo:lo + half] = ((t1 * cos - t2 * sin) * scale).astype(BF16)
            proj_ref[:, lo + half:lo + RET_KEY_DIM] = ((t1 * sin + t2 * cos) * scale).astype(BF16)

    @pl.when(j == BLK_AQ)
    def _():
        proj_ref[...] = (acc * (DIFF_HEAD_DIM ** -0.5 * LOG2_E)).astype(BF16)

    @pl.when((j == BLK_AK) | (j == BLK_AV) | (j == BLK_RV))
    def _():
        proj_ref[...] = acc.astype(BF16)

    @pl.when(j == BLK_RQ)
    def _():
        rotary(1.0)

    @pl.when(j == BLK_RK)
    def _():
        rotary(RET_KEY_DIM ** -0.5)

    @pl.when(j == BLK_RG)
    def _():
        gate_ref[...] = acc


def _in_proj(x, g, w, cos, sin, *, seq, tm):
    t, d = x.shape
    n_blk = w.shape[1] // PROJ_BLOCK
    assert n_blk == 7 and t % tm == 0 and seq % tm == 0
    tiles_per_seq = seq // tm
    return pl.pallas_call(
        _in_proj_kernel,
        grid=(t // tm, n_blk),
        in_specs=[
            pl.BlockSpec((tm, d), lambda i, j: (i, 0)),
            pl.BlockSpec((1, d), lambda i, j: (0, 0)),
            pl.BlockSpec((d, PROJ_BLOCK), lambda i, j: (0, j)),
            pl.BlockSpec((tm, RET_KEY_DIM // 2), lambda i, j: (i % tiles_per_seq, 0)),
            pl.BlockSpec((tm, RET_KEY_DIM // 2), lambda i, j: (i % tiles_per_seq, 0)),
        ],
        out_specs=[
            pl.BlockSpec((tm, PROJ_BLOCK), lambda i, j: (i, jnp.minimum(j, BLK_RV))),
            pl.BlockSpec((tm, PROJ_BLOCK), lambda i, j: (i, 0)),
        ],
        out_shape=[
            jax.ShapeDtypeStruct((t, BLK_RG * PROJ_BLOCK), BF16),
            jax.ShapeDtypeStruct((t, PROJ_BLOCK), F32),
        ],
        scratch_shapes=[pltpu.VMEM((tm, d), BF16)],
        compiler_params=_params("arbitrary", "arbitrary"),
        name="in_proj",
    )(x, g, w, cos, sin)


def _t5_bucket(dist):
    max_exact = NUM_BUCKETS // 2
    is_small = dist < max_exact
    df = jnp.maximum(dist, 1).astype(F32)
    large = max_exact + (jnp.log(df / max_exact) / math.log(MAX_DISTANCE / max_exact)
                         * (NUM_BUCKETS - max_exact)).astype(jnp.int32)
    large = jnp.minimum(large, NUM_BUCKETS - 1)
    return jnp.where(is_small, dist, large)


def _attn_kernel(qi_ref, ki_ref, q_ref, k_ref, v_ref, bias_ref, lam_ref, subln_ref, o_ref,
                 acc_ref, m_ref, l_ref, band_ref, *, ta, lam_init):
    b = pl.program_id(0)
    p = pl.program_id(1)
    qi = qi_ref[p]
    ki = ki_ref[p]
    n_sub = ta // LANES

    @pl.when((b == 0) & (p == 0))
    def _():
        row = lax.broadcasted_iota(jnp.int32, (LANES, LANES), 0)
        col = lax.broadcasted_iota(jnp.int32, (LANES, LANES), 1)
        for slot in range(2):
            bucket = _t5_bucket(jnp.maximum(row - col + slot * LANES, 0))
            for m in range(N_DIFF_MAPS):
                val = jnp.zeros((LANES, LANES), F32)
                for bk in range(NUM_BUCKETS):
                    val = jnp.where(bucket == bk, bias_ref[bk, m], val)
                band_ref[slot, m] = (val - bias_ref[NUM_BUCKETS - 1, m]) * LOG2_E

    @pl.when(ki == 0)
    def _():
        m_ref[...] = jnp.full(m_ref.shape, NEG_INF, F32)
        l_ref[...] = jnp.zeros(l_ref.shape, F32)
        acc_ref[...] = jnp.zeros(acc_ref.shape, F32)

    def biased_slabs(s, m, kind):
        slabs = []
        for c in range(n_sub):
            sc = s[:, c * LANES:(c + 1) * LANES]
            if kind == "far" or (kind == "prev" and c != n_sub - 1):
                slabs.append(sc)
                continue
            pieces = []
            for a in range(n_sub):
                blk = sc[a * LANES:(a + 1) * LANES]
                if kind == "prev":
                    if a == 0:
                        blk = blk + band_ref[1, m]
                elif a < c:
                    blk = jnp.full((LANES, LANES), NEG_INF, F32)
                elif a == c:
                    row = lax.broadcasted_iota(jnp.int32, (LANES, LANES), 0)
                    col = lax.broadcasted_iota(jnp.int32, (LANES, LANES), 1)
                    blk = jnp.where(row >= col, blk + band_ref[0, m], NEG_INF)
                elif a == c + 1:
                    blk = blk + band_ref[1, m]
                pieces.append(blk)
            slabs.append(jnp.concatenate(pieces, axis=0))
        return slabs

    def scores(m):
        q_m = q_ref[:, m * DIFF_HEAD_DIM:(m + 1) * DIFF_HEAD_DIM]
        k_m = k_ref[:, m * DIFF_HEAD_DIM:(m + 1) * DIFF_HEAD_DIM]
        return lax.dot_general(q_m, k_m, (((1,), (1,)), ((), ())), preferred_element_type=F32)

    def step(kind):
        for m in range(N_DIFF_MAPS):
            s = scores(m)
            h = m // 2
            v_h = v_ref[:, h * DIFF_V_DIM:(h + 1) * DIFF_V_DIM]
            slabs = biased_slabs(s, m, kind)
            part_max = slabs[0]
            for sc in slabs[1:]:
                part_max = jnp.maximum(part_max, sc)
            m_prev = m_ref[m]
            m_new = jnp.maximum(m_prev, jnp.max(part_max, axis=1, keepdims=True))
            alpha = jnp.exp2(m_prev - m_new)
            probs = [jnp.exp2(sc - m_new) for sc in slabs]
            part_sum = probs[0]
            for pc in probs[1:]:
                part_sum = part_sum + pc
            m_ref[m] = m_new
            l_ref[m] = alpha * l_ref[m] + part_sum
            pv = jnp.dot(jnp.concatenate([pc.astype(BF16) for pc in probs], axis=1), v_h,
                         preferred_element_type=F32)
            alpha_v = jnp.concatenate([alpha] * (DIFF_V_DIM // LANES), axis=1)
            acc_ref[m] = acc_ref[m] * alpha_v + pv

    @pl.when(ki < qi - 1)
    def _():
        step("far")

    @pl.when((ki == qi - 1))
    def _():
        step("prev")

    @pl.when(ki == qi)
    def _():
        step("diag")
        lam_p = lam_ref[...]
        lam = (jnp.exp(jnp.sum(lam_p[0:1] * lam_p[1:2], axis=1, keepdims=True))
               - jnp.exp(jnp.sum(lam_p[2:3] * lam_p[3:4], axis=1, keepdims=True)) + lam_init)
        for h in range(N_DIFF_HEADS):
            inv0 = 1.0 / jnp.sum(l_ref[2 * h], axis=1, keepdims=True)
            inv1 = 1.0 / jnp.sum(l_ref[2 * h + 1], axis=1, keepdims=True)
            o = acc_ref[2 * h] * inv0 - lam * (acc_ref[2 * h + 1] * inv1)
            o = _rms_norm(o, subln_ref[...]) * (1.0 - lam_init)
            o_ref[:, h * DIFF_V_DIM:(h + 1) * DIFF_V_DIM] = o.astype(BF16)


def _diff_attention(proj, rel_bias, lam_params, subln, *, batch, seq, ta, lam_init):
    t = proj.shape[0]
    assert seq % ta == 0 and ta % LANES == 0 and ta >= 2 * LANES
    nq = seq // ta
    pairs = [(q, k) for q in range(nq) for k in range(q + 1)]
    qi = jnp.asarray([q for q, _ in pairs], jnp.int32)
    ki = jnp.asarray([k for _, k in pairs], jnp.int32)
    width = N_DIFF_HEADS * DIFF_V_DIM
    grid_spec = pltpu.PrefetchScalarGridSpec(
        num_scalar_prefetch=2,
        grid=(batch, len(pairs)),
        in_specs=[
            pl.BlockSpec((ta, PROJ_BLOCK), lambda b, p, qi, ki: (b * nq + qi[p], BLK_AQ)),
            pl.BlockSpec((ta, PROJ_BLOCK), lambda b, p, qi, ki: (b * nq + ki[p], BLK_AK)),
            pl.BlockSpec((ta, PROJ_BLOCK), lambda b, p, qi, ki: (b * nq + ki[p], BLK_AV)),
            pl.BlockSpec(memory_space=pltpu.SMEM),
            pl.BlockSpec((4, DIFF_HEAD_DIM), lambda b, p, qi, ki: (0, 0)),
            pl.BlockSpec((1, DIFF_V_DIM), lambda b, p, qi, ki: (0, 0)),
        ],
        out_specs=pl.BlockSpec((ta, width), lambda b, p, qi, ki: (b * nq + qi[p], 0)),
        scratch_shapes=[
            pltpu.VMEM((N_DIFF_MAPS, ta, DIFF_V_DIM), F32),
            pltpu.VMEM((N_DIFF_MAPS, ta, LANES), F32),
            pltpu.VMEM((N_DIFF_MAPS, ta, LANES), F32),
            pltpu.VMEM((2, N_DIFF_MAPS, LANES, LANES), F32),
        ],
    )
    return pl.pallas_call(
        functools.partial(_attn_kernel, ta=ta, lam_init=lam_init),
        grid_spec=grid_spec,
        out_shape=jax.ShapeDtypeStruct((t, width), BF16),
        compiler_params=_params("arbitrary", "arbitrary"),
        name="diff_attention",
    )(qi, ki, proj, proj, proj, rel_bias, lam_params, subln)


def _ret_log_decay(h):
    return math.log(1.0 - 2.0 ** (-5.0 - h))


def _retention_kernel(q_ref, k_ref, v_ref, g_ref, o_ref, state_ref, mask_ref, qdec_ref, kdec_ref, *, chunk):
    b = pl.program_id(0)
    c = pl.program_id(1)

    @pl.when((b == 0) & (c == 0))
    def _():
        rel = (lax.broadcasted_iota(jnp.int32, (chunk, chunk), 0)
               - lax.broadcasted_iota(jnp.int32, (chunk, chunk), 1)).astype(F32)
        idx = lax.broadcasted_iota(jnp.int32, (chunk, RET_KEY_DIM), 0).astype(F32)
        for h in range(N_RET_HEADS):
            log_g = _ret_log_decay(h)
            mask_ref[h] = jnp.where(rel >= 0, jnp.exp(jnp.maximum(rel, 0.0) * log_g), 0.0)
            qdec_ref[h] = jnp.exp((idx + 1.0) * log_g)
            kdec_ref[h] = jnp.exp((chunk - 1.0 - idx) * log_g)

    @pl.when(c == 0)
    def _():
        state_ref[...] = jnp.zeros(state_ref.shape, F32)

    for h in range(N_RET_HEADS):
        sl = slice(h * RET_KEY_DIM, (h + 1) * RET_KEY_DIM)
        q = q_ref[:, sl]
        k = k_ref[:, sl]
        v = v_ref[:, sl]
        scores = lax.dot_general(q, k, (((1,), (1,)), ((), ())), preferred_element_type=F32)
        inner = jnp.dot((scores * mask_ref[h]).astype(BF16), v, preferred_element_type=F32)
        state = state_ref[h]
        cross = jnp.dot(q, state.astype(BF16), preferred_element_type=F32) * qdec_ref[h]
        k_dec = (k.astype(F32) * kdec_ref[h]).astype(BF16)
        state_ref[h] = (state * math.exp(chunk * _ret_log_decay(h))
                        + lax.dot_general(k_dec, v, (((0,), (0,)), ((), ())), preferred_element_type=F32))
        o = inner + cross
        o = o * lax.rsqrt(jnp.mean(o * o, axis=-1, keepdims=True) + EPS)
        o_ref[:, sl] = (o * _silu(g_ref[:, sl])).astype(BF16)


def _retention(proj, gate, *, batch, seq, chunk):
    t = proj.shape[0]
    assert seq % chunk == 0
    nc = seq // chunk
    width = N_RET_HEADS * RET_V_DIM
    return pl.pallas_call(
        functools.partial(_retention_kernel, chunk=chunk),
        grid=(batch, nc),
        in_specs=[
            pl.BlockSpec((chunk, PROJ_BLOCK), lambda b, c: (b * nc + c, BLK_RQ)),
            pl.BlockSpec((chunk, PROJ_BLOCK), lambda b, c: (b * nc + c, BLK_RK)),
            pl.BlockSpec((chunk, PROJ_BLOCK), lambda b, c: (b * nc + c, BLK_RV)),
            pl.BlockSpec((chunk, PROJ_BLOCK), lambda b, c: (b * nc + c, 0)),
        ],
        out_specs=pl.BlockSpec((chunk, width), lambda b, c: (b * nc + c, 0)),
        out_shape=jax.ShapeDtypeStruct((t, width), BF16),
        scratch_shapes=[
            pltpu.VMEM((N_RET_HEADS, RET_KEY_DIM, RET_V_DIM), F32),
            pltpu.VMEM((N_RET_HEADS, chunk, chunk), F32),
            pltpu.VMEM((N_RET_HEADS, chunk, RET_KEY_DIM), F32),
            pltpu.VMEM((N_RET_HEADS, chunk, RET_KEY_DIM), F32),
        ],
        compiler_params=_params("arbitrary", "arbitrary"),
        name="retention",
    )(proj, proj, proj, gate)


def _out_proj_kernel(x_ref, a_ref, r_ref, w_ref, o_ref):
    ka = a_ref.shape[1]
    y = jnp.dot(a_ref[...], w_ref[:ka, :], preferred_element_type=F32)
    y = y + jnp.dot(r_ref[...], w_ref[ka:, :], preferred_element_type=F32)
    o_ref[...] = x_ref[...] + y


def _out_proj(x, a, r, w, *, tm, tn):
    t, d = x.shape
    ka, kr = a.shape[1], r.shape[1]
    assert t % tm == 0 and d % tn == 0 and w.shape == (ka + kr, d)
    return pl.pallas_call(
        _out_proj_kernel,
        grid=(t // tm, d // tn),
        in_specs=[
            pl.BlockSpec((tm, tn), lambda i, j: (i, j)),
            pl.BlockSpec((tm, ka), lambda i, j: (i, 0)),
            pl.BlockSpec((tm, kr), lambda i, j: (i, 0)),
            pl.BlockSpec((ka + kr, tn), lambda i, j: (0, j)),
        ],
        out_specs=pl.BlockSpec((tm, tn), lambda i, j: (i, j)),
        out_shape=jax.ShapeDtypeStruct((t, d), F32),
        compiler_params=_params("arbitrary", "arbitrary"),
        name="out_proj",
    )(x, a, r, w)


def _pool_kernel(x_ref, xp_ref, g_ref, w_ref, scale_ref, o_ref, h_ref, *, tm, seq):
    i = pl.program_id(0)
    start = (i * tm) % seq
    g = g_ref[...]
    h_prev = _rms_norm(xp_ref[...], g)
    h_ref[0:POOL_HALO, :] = jnp.where(start == 0, 0.0, h_prev)
    h_ref[POOL_HALO:, :] = _rms_norm(x_ref[...], g)
    pos = start + lax.broadcasted_iota(jnp.int32, (tm, 1), 0)
    gd = w_ref.shape[1]
    for gi, win in enumerate(POOL_WINDOWS):
        cols = slice(gi * gd, (gi + 1) * gd)
        total = h_ref[:, cols]
        hg = total[POOL_HALO:]
        span = 1
        while span < win:
            total = total + pltpu.roll(total, span, 0)
            span *= 2
        inv_count = 1.0 / jnp.minimum(pos + 1, win).astype(F32)
        pooled = total[POOL_HALO:] * inv_count - hg
        y = jnp.dot(pooled.astype(BF16), w_ref[gi], preferred_element_type=F32)
        o_ref[:, cols] = x_ref[:, cols] + y * scale_ref[:, cols]


def _pool_mixer(x, g, w, scale, *, seq, tm):
    t, d = x.shape
    assert t % tm == 0 and seq % tm == 0 and tm % POOL_HALO == 0 and max(POOL_WINDOWS) <= POOL_HALO
    assert all(win & (win - 1) == 0 for win in POOL_WINDOWS)
    halo_blocks = tm // POOL_HALO
    return pl.pallas_call(
        functools.partial(_pool_kernel, tm=tm, seq=seq),
        grid=(t // tm,),
        in_specs=[
            pl.BlockSpec((tm, d), lambda i: (i, 0)),
            pl.BlockSpec((POOL_HALO, d), lambda i: (jnp.maximum(i * halo_blocks - 1, 0), 0)),
            pl.BlockSpec((1, d), lambda i: (0, 0)),
            pl.BlockSpec(w.shape, lambda i: (0, 0, 0)),
            pl.BlockSpec((1, d), lambda i: (0, 0)),
        ],
        out_specs=pl.BlockSpec((tm, d), lambda i: (i, 0)),
        out_shape=jax.ShapeDtypeStruct((t, d), F32),
        scratch_shapes=[pltpu.VMEM((tm + POOL_HALO, d), F32)],
        compiler_params=_params("arbitrary"),
        name="pool_mixer",
    )(x, x, g, w, scale)


def _conv(u, prev, cw_ref, cb_ref):
    ext = jnp.concatenate([prev, u], axis=0)
    y = cb_ref[...] + u * cw_ref[CONV_WIDTH - 1:CONV_WIDTH, :]
    for tap in range(1, CONV_WIDTH):
        shifted = pltpu.roll(ext, tap, 0)[CONV_HALO:]
        y = y + shifted * cw_ref[CONV_WIDTH - 1 - tap:CONV_WIDTH - tap, :]
    return y


def _gated_chunk(h_ref, w_refs, tail_v_ref, tail_g_ref, f, *, tsub, n_sub, silu, consume):
    wv_ref, wg_ref, cwv_ref, cwg_ref, cbv_ref, cbg_ref = w_refs

    def up(s):
        h_s = h_ref[s * tsub:(s + 1) * tsub, :]
        return (jnp.dot(h_s, wv_ref[...], preferred_element_type=F32),
                jnp.dot(h_s, wg_ref[...], preferred_element_type=F32))

    prev = (tail_v_ref[f], tail_g_ref[f])
    u_cur = up(0)
    for s in range(n_sub):
        u_next = up(s + 1) if s + 1 < n_sub else None
        val = _conv(u_cur[0], prev[0], cwv_ref, cbv_ref)
        gate = _conv(u_cur[1], prev[1], cwg_ref, cbg_ref)
        consume(s, (silu(gate) * val).astype(BF16))
        prev = (u_cur[0][tsub - CONV_HALO:], u_cur[1][tsub - CONV_HALO:])
        u_cur = u_next
    tail_v_ref[f] = prev[0]
    tail_g_ref[f] = prev[1]


def _ffn_kernel(*refs, tm, seq, final, n_sub, silu):
    if final:
        (x_ref, g_ref, wv_ref, wg_ref, cwv_ref, cwg_ref, cbv_ref, cbg_ref, wd_ref, gf_ref,
         o_ref, h_ref, tail_v_ref, tail_g_ref) = refs
    else:
        (x_ref, g_ref, wv_ref, wg_ref, cwv_ref, cwg_ref, cbv_ref, cbg_ref, wd_ref,
         o_ref, h_ref, tail_v_ref, tail_g_ref) = refs
    i = pl.program_id(0)
    f = pl.program_id(1)
    n_f = pl.num_programs(1)
    tsub = tm // n_sub

    @pl.when(f == 0)
    def _():
        x = x_ref[...]
        h_ref[...] = _rms_norm(x, g_ref[...]).astype(BF16)
        o_ref[...] = x

    @pl.when((i * tm) % seq == 0)
    def _():
        tail_v_ref[f] = jnp.zeros(tail_v_ref.shape[1:], F32)
        tail_g_ref[f] = jnp.zeros(tail_g_ref.shape[1:], F32)

    def consume(s, act):
        o_ref[s * tsub:(s + 1) * tsub, :] += jnp.dot(act, wd_ref[...], preferred_element_type=F32)

    _gated_chunk(h_ref, (wv_ref, wg_ref, cwv_ref, cwg_ref, cbv_ref, cbg_ref), tail_v_ref, tail_g_ref, f,
                 tsub=tsub, n_sub=n_sub, silu=silu, consume=consume)

    if final:
        @pl.when(f == n_f - 1)
        def _():
            o_ref[...] = _rms_norm(o_ref[...], gf_ref[...])


def _conv_ffn(x, g, w_up, conv_w, conv_b, w_down, final_g, *, seq, tm, tf, n_sub, silu, name):
    t, d = x.shape
    d_ff = w_down.shape[0]
    assert t % tm == 0 and seq % tm == 0 and d_ff % tf == 0 and w_up.shape == (d, 2 * d_ff)
    n_f = d_ff // tf
    assert tm % n_sub == 0 and (tm // n_sub) % 16 == 0
    final = final_g is not None
    in_specs = [
        pl.BlockSpec((tm, d), lambda i, f: (i, 0)),
        pl.BlockSpec((1, d), lambda i, f: (0, 0)),
        pl.BlockSpec((d, tf), lambda i, f: (0, f)),
        pl.BlockSpec((d, tf), lambda i, f: (0, n_f + f)),
        pl.BlockSpec((CONV_WIDTH, tf), lambda i, f: (0, f)),
        pl.BlockSpec((CONV_WIDTH, tf), lambda i, f: (0, n_f + f)),
        pl.BlockSpec((1, tf), lambda i, f: (0, f)),
        pl.BlockSpec((1, tf), lambda i, f: (0, n_f + f)),
        pl.BlockSpec((tf, d), lambda i, f: (f, 0)),
    ]
    args = [x, g, w_up, w_up, conv_w, conv_w, conv_b, conv_b, w_down]
    if final:
        in_specs.append(pl.BlockSpec((1, d), lambda i, f: (0, 0)))
        args.append(final_g)
    return pl.pallas_call(
        functools.partial(_ffn_kernel, tm=tm, seq=seq, final=final, n_sub=n_sub, silu=silu),
        grid=(t // tm, n_f),
        in_specs=in_specs,
        out_specs=pl.BlockSpec((tm, d), lambda i, f: (i, 0)),
        out_shape=jax.ShapeDtypeStruct((t, d), F32),
        scratch_shapes=[
            pltpu.VMEM((tm, d), BF16),
            pltpu.VMEM((n_f, CONV_HALO, tf), F32),
            pltpu.VMEM((n_f, CONV_HALO, tf), F32),
        ],
        compiler_params=_params("arbitrary", "arbitrary"),
        name=name,
    )(*args)


def _ffn2_kernel(x_ref, xres_ref, g_ref, wv_ref, wg_ref, cwv_ref, cwg_ref, cbv_ref, cbg_ref, wd_ref,
                 o_ref, h_ref, act_ref, tail_v_ref, tail_g_ref, *, tm, seq, n_sub, n_f, tf, silu):
    i = pl.program_id(0)
    f = pl.program_id(1)
    tsub = tm // n_sub

    @pl.when(f == 0)
    def _():
        h_ref[...] = _rms_norm(x_ref[...], g_ref[...]).astype(BF16)

    @pl.when(f < n_f)
    def _():
        @pl.when((i * tm) % seq == 0)
        def _():
            tail_v_ref[f] = jnp.zeros(tail_v_ref.shape[1:], F32)
            tail_g_ref[f] = jnp.zeros(tail_g_ref.shape[1:], F32)

        acts = []
        _gated_chunk(h_ref, (wv_ref, wg_ref, cwv_ref, cwg_ref, cbv_ref, cbg_ref), tail_v_ref, tail_g_ref, f,
                     tsub=tsub, n_sub=n_sub, silu=silu, consume=lambda s, act: acts.append(act))
        act = jnp.concatenate(acts, axis=0)
        for c in range(n_f):
            @pl.when(f == c)
            def _():
                act_ref[:, c * tf:(c + 1) * tf] = act

    @pl.when(f >= n_f)
    def _():
        o_ref[...] = xres_ref[...] + jnp.dot(act_ref[...], wd_ref[...], preferred_element_type=F32)


def _conv_ffn_two_phase(x, g, w_up, conv_w, conv_b, w_down, *, seq, tm, tf, tn, n_sub, silu, name):
    t, d = x.shape
    d_ff = w_down.shape[0]
    assert t % tm == 0 and seq % tm == 0 and d_ff % tf == 0 and d % tn == 0 and w_up.shape == (d, 2 * d_ff)
    n_f = d_ff // tf
    n_d = d // tn
    assert tm % n_sub == 0 and (tm // n_sub) % 16 == 0
    up_f = lambda f: jnp.minimum(f, n_f - 1)
    down_c = lambda f: jnp.maximum(f - n_f, 0)
    return pl.pallas_call(
        functools.partial(_ffn2_kernel, tm=tm, seq=seq, n_sub=n_sub, n_f=n_f, tf=tf, silu=silu),
        grid=(t // tm, n_f + n_d),
        in_specs=[
            pl.BlockSpec((tm, d), lambda i, f: (i, 0)),
            pl.BlockSpec((tm, tn), lambda i, f: (i, down_c(f))),
            pl.BlockSpec((1, d), lambda i, f: (0, 0)),
            pl.BlockSpec((d, tf), lambda i, f: (0, up_f(f))),
            pl.BlockSpec((d, tf), lambda i, f: (0, n_f + up_f(f))),
            pl.BlockSpec((CONV_WIDTH, tf), lambda i, f: (0, up_f(f))),
            pl.BlockSpec((CONV_WIDTH, tf), lambda i, f: (0, n_f + up_f(f))),
            pl.BlockSpec((1, tf), lambda i, f: (0, up_f(f))),
            pl.BlockSpec((1, tf), lambda i, f: (0, n_f + up_f(f))),
            pl.BlockSpec((d_ff, tn), lambda i, f: (0, down_c(f))),
        ],
        out_specs=pl.BlockSpec((tm, tn), lambda i, f: (i, down_c(f))),
        out_shape=jax.ShapeDtypeStruct((t, d), F32),
        scratch_shapes=[
            pltpu.VMEM((tm, d), BF16),
            pltpu.VMEM((tm, d_ff), BF16),
            pltpu.VMEM((n_f, CONV_HALO, tf), F32),
            pltpu.VMEM((n_f, CONV_HALO, tf), F32),
        ],
        compiler_params=_params("arbitrary", "arbitrary"),
        name=name,
    )(x, x, g, w_up, w_up, conv_w, conv_w, conv_b, conv_b, w_down)


def _tiles(seq):
    pick = lambda want: min(want, seq)
    return dict(proj_tm=pick(1024), attn_ta=pick(512), ret_chunk=pick(512), out_tm=pick(1024),
                pool_tm=pick(512), ffn_tm=pick(1024))


def kernel(x, rel_bias, mix_norm, ffn_norm, final_norm, w_in, w_out, lam_q1, lam_k1, lam_q2, lam_k2,
           diff_subln, pool_w, pool_scale, w_up, conv_w, conv_b, w_down):
    batch, seq, d = x.shape
    depth = mix_norm.shape[0]
    tiles = _tiles(seq)
    xf = x.reshape(batch * seq, d)

    pos = jnp.arange(seq, dtype=F32)
    inv_freq = 1.0 / (ROPE_BASE ** jnp.linspace(0.0, 1.0, RET_KEY_DIM // 2, dtype=F32))
    ang = pos[:, None] * inv_freq[None, :]
    cos, sin = jnp.cos(ang), jnp.sin(ang)

    for l in range(depth):
        g_mix = mix_norm[l][None, :]
        if l % 2 == 0:
            i = l // 2
            lam_init = 0.8 - 0.6 * math.exp(-0.3 * l)
            proj, gate = _in_proj(xf, g_mix, w_in[i].astype(BF16), cos, sin, seq=seq, tm=tiles["proj_tm"])
            lam_params = jnp.stack([lam_q1[i], lam_k1[i], lam_q2[i], lam_k2[i]]).astype(F32)
            a_out = _diff_attention(proj, rel_bias, lam_params, diff_subln[i][None, :], batch=batch, seq=seq,
                                    ta=tiles["attn_ta"], lam_init=lam_init)
            r_out = _retention(proj, gate, batch=batch, seq=seq,
                               chunk=tiles["ret_chunk"] if l == 0 else min(256, seq))
            xf = _out_proj(xf, a_out, r_out, w_out[i].astype(BF16), tm=tiles["out_tm"], tn=PROJ_BLOCK)
        else:
            j = l // 2
            xf = _pool_mixer(xf, g_mix, pool_w[j].astype(BF16), pool_scale[j][None, :], seq=seq,
                             tm=tiles["pool_tm"] if l == 1 else min(1024, seq))
        ffn_args = (xf, ffn_norm[l][None, :], w_up[l].astype(BF16), conv_w[l], conv_b[l][None, :],
                    w_down[l].astype(BF16))
        small_tm = min(512, seq)
        if l == 0:
            xf = _conv_ffn(*ffn_args, None, seq=seq, tm=tiles["ffn_tm"], tf=512, n_sub=4, silu=_silu_tanh,
                           name="ffn_tm1024_tanh")
        elif l == 1:
            xf = _conv_ffn(*ffn_args, None, seq=seq, tm=small_tm, tf=512, n_sub=2, silu=_silu_tanh,
                           name="ffn_tm512_tanh")
        elif l == 2:
            xf = _conv_ffn_two_phase(*ffn_args, seq=seq, tm=small_tm, tf=512, tn=512, n_sub=4, silu=_silu_tanh,
                                     name="ffn_two_phase")
        else:
            xf = _conv_ffn(*ffn_args, final_norm[None, :], seq=seq, tm=tiles["ffn_tm"], tf=512, n_sub=4,
                           silu=_silu_tanh, name="ffn_tm1024_tanh_final")
    return xf.reshape(batch, seq, d)
```

```python
import functools
import math

import jax
import jax.numpy as jnp
from jax import lax
from jax.experimental import pallas as pl
from jax.experimental.pallas import tpu as pltpu

F32 = jnp.float32
BF16 = jnp.bfloat16

N_DIFF_HEADS = 4
DIFF_HEAD_DIM = 128
DIFF_V_DIM = 2 * DIFF_HEAD_DIM
N_DIFF_MAPS = 2 * N_DIFF_HEADS
N_RET_HEADS = 4
RET_KEY_DIM = 256
RET_V_DIM = 256
ROPE_BASE = 10000.0
NUM_BUCKETS = 32
MAX_DISTANCE = 128
POOL_WINDOWS = (2, 4, 8, 16)
POOL_HALO = 16
CONV_WIDTH = 3
CONV_HALO = 8
EPS = 1e-6
NEG_INF = -1e30
LOG2_E = math.log2(math.e)
PROJ_BLOCK = 1024
(BLK_AQ, BLK_AK, BLK_AV, BLK_RQ, BLK_RK, BLK_RV, BLK_RG) = range(7)

LANES = 128
VMEM_LIMIT = 56 * 1024 * 1024


def _params(*semantics):
    return pltpu.CompilerParams(dimension_semantics=semantics, vmem_limit_bytes=VMEM_LIMIT)


def _rms_norm(xf, g):
    y = xf * lax.rsqrt(jnp.mean(xf * xf, axis=-1, keepdims=True) + EPS)
    return y * g


def _silu(z):
    half = 0.5 * z
    return half + half * jnp.tanh(half)


def _in_proj_kernel(x_ref, g_ref, w_ref, cos_ref, sin_ref, proj_ref, gate_ref, h_ref, *, n_sub):
    j = pl.program_id(1)
    tsub = h_ref.shape[0] // n_sub

    @pl.when(j == 0)
    def _():
        h_ref[...] = _rms_norm(x_ref[...], g_ref[...]).astype(BF16)

    def matmul(s):
        return jnp.dot(h_ref[s * tsub:(s + 1) * tsub, :], w_ref[...], preferred_element_type=F32)

    def plain(scale):
        def write(rows, acc):
            proj_ref[rows, :] = (acc if scale is None else acc * scale).astype(BF16)
        return write

    def rotary(scale):
        def write(rows, acc):
            cos = cos_ref[rows, :]
            sin = sin_ref[rows, :]
            half = RET_KEY_DIM // 2
            for h in range(N_RET_HEADS):
                lo = h * RET_KEY_DIM
                t1 = acc[:, lo:lo + half]
                t2 = acc[:, lo + half:lo + RET_KEY_DIM]
                proj_ref[rows, lo:lo + half] = ((t1 * cos - t2 * sin) * scale).astype(BF16)
                proj_ref[rows, lo + half:lo + RET_KEY_DIM] = ((t1 * sin + t2 * cos) * scale).astype(BF16)
        return write

    def gate(rows, acc):
        gate_ref[rows, :] = acc

    acc0 = matmul(0)

    def finish(write):
        acc = acc0
        for s in range(n_sub):
            acc_next = matmul(s + 1) if s + 1 < n_sub else None
            write(slice(s * tsub, (s + 1) * tsub), acc)
            acc = acc_next

    @pl.when(j == BLK_AQ)
    def _():
        finish(plain(DIFF_HEAD_DIM ** -0.5 * LOG2_E))

    @pl.when((j == BLK_AK) | (j == BLK_AV) | (j == BLK_RV))
    def _():
        finish(plain(None))

    @pl.when(j == BLK_RQ)
    def _():
        finish(rotary(1.0))

    @pl.when(j == BLK_RK)
    def _():
        finish(rotary(RET_KEY_DIM ** -0.5))

    @pl.when(j == BLK_RG)
    def _():
        finish(gate)


def _in_proj(x, g, w, layer, cos, sin, *, seq, tm, n_sub):
    t, d = x.shape
    n_blk = w.shape[2] // PROJ_BLOCK
    assert n_blk == 7 and t % tm == 0 and seq % tm == 0 and tm % (16 * n_sub) == 0
    tiles_per_seq = seq // tm
    return pl.pallas_call(
        functools.partial(_in_proj_kernel, n_sub=n_sub),
        grid=(t // tm, n_blk),
        in_specs=[
            pl.BlockSpec((tm, d), lambda i, j: (i, 0)),
            pl.BlockSpec((1, d), lambda i, j: (0, 0)),
            pl.BlockSpec((None, d, PROJ_BLOCK), lambda i, j: (layer, 0, j)),
            pl.BlockSpec((tm, RET_KEY_DIM // 2), lambda i, j: (i % tiles_per_seq, 0)),
            pl.BlockSpec((tm, RET_KEY_DIM // 2), lambda i, j: (i % tiles_per_seq, 0)),
        ],
        out_specs=[
            pl.BlockSpec((tm, PROJ_BLOCK), lambda i, j: (i, jnp.minimum(j, BLK_RV))),
            pl.BlockSpec((tm, PROJ_BLOCK), lambda i, j: (i, 0)),
        ],
        out_shape=[
            jax.ShapeDtypeStruct((t, BLK_RG * PROJ_BLOCK), BF16),
            jax.ShapeDtypeStruct((t, PROJ_BLOCK), F32),
        ],
        scratch_shapes=[pltpu.VMEM((tm, d), BF16)],
        compiler_params=_params("arbitrary", "arbitrary"),
        name="in_proj",
    )(x, g, w, cos, sin)


def _t5_bucket(dist):
    max_exact = NUM_BUCKETS // 2
    is_small = dist < max_exact
    df = jnp.maximum(dist, 1).astype(F32)
    large = max_exact + (jnp.log(df / max_exact) / math.log(MAX_DISTANCE / max_exact)
                         * (NUM_BUCKETS - max_exact)).astype(jnp.int32)
    large = jnp.minimum(large, NUM_BUCKETS - 1)
    return jnp.where(is_small, dist, large)


def _attn_kernel(qi_ref, ki_ref, q_ref, k_ref, v_ref, bias_ref, lam_ref, subln_ref, o_ref,
                 acc_ref, m_ref, l_ref, band_ref, *, ta, lam_init):
    b = pl.program_id(0)
    p = pl.program_id(1)
    qi = qi_ref[p]
    ki = ki_ref[p]
    n_sub = ta // LANES

    @pl.when((b == 0) & (p == 0))
    def _():
        row = lax.broadcasted_iota(jnp.int32, (LANES, LANES), 0)
        col = lax.broadcasted_iota(jnp.int32, (LANES, LANES), 1)
        for slot in range(2):
            bucket = _t5_bucket(jnp.maximum(row - col + slot * LANES, 0))
            for m in range(N_DIFF_MAPS):
                val = jnp.zeros((LANES, LANES), F32)
                for bk in range(NUM_BUCKETS):
                    val = jnp.where(bucket == bk, bias_ref[bk, m], val)
                band_ref[slot, m] = (val - bias_ref[NUM_BUCKETS - 1, m]) * LOG2_E

    @pl.when(ki == 0)
    def _():
        m_ref[...] = jnp.full(m_ref.shape, NEG_INF, F32)
        l_ref[...] = jnp.zeros(l_ref.shape, F32)
        acc_ref[...] = jnp.zeros(acc_ref.shape, F32)

    def biased_slabs(s, m, kind):
        slabs = []
        for c in range(n_sub):
            sc = s[:, c * LANES:(c + 1) * LANES]
            if kind == "far" or (kind == "prev" and c != n_sub - 1):
                slabs.append(sc)
                continue
            pieces = []
            for a in range(n_sub):
                blk = sc[a * LANES:(a + 1) * LANES]
                if kind == "prev":
                    if a == 0:
                        blk = blk + band_ref[1, m]
                elif a < c:
                    blk = jnp.full((LANES, LANES), NEG_INF, F32)
                elif a == c:
                    row = lax.broadcasted_iota(jnp.int32, (LANES, LANES), 0)
                    col = lax.broadcasted_iota(jnp.int32, (LANES, LANES), 1)
                    blk = jnp.where(row >= col, blk + band_ref[0, m], NEG_INF)
                elif a == c + 1:
                    blk = blk + band_ref[1, m]
                pieces.append(blk)
            slabs.append(jnp.concatenate(pieces, axis=0))
        return slabs

    def scores(m):
        q_m = q_ref[:, m * DIFF_HEAD_DIM:(m + 1) * DIFF_HEAD_DIM]
        k_m = k_ref[:, m * DIFF_HEAD_DIM:(m + 1) * DIFF_HEAD_DIM]
        return lax.dot_general(q_m, k_m, (((1,), (1,)), ((), ())), preferred_element_type=F32)

    def step(kind):
        for m in range(N_DIFF_MAPS):
            s = scores(m)
            h = m // 2
            v_h = v_ref[:, h * DIFF_V_DIM:(h + 1) * DIFF_V_DIM]
            slabs = biased_slabs(s, m, kind)
            part_max = slabs[0]
            for sc in slabs[1:]:
                part_max = jnp.maximum(part_max, sc)
            m_prev = m_ref[m]
            m_new = jnp.maximum(m_prev, jnp.max(part_max, axis=1, keepdims=True))
            alpha = jnp.exp2(m_prev - m_new)
            probs = [jnp.exp2(sc - m_new) for sc in slabs]
            part_sum = probs[0]
            for pc in probs[1:]:
                part_sum = part_sum + pc
            m_ref[m] = m_new
            l_ref[m] = alpha * l_ref[m] + part_sum
            pv = jnp.dot(jnp.concatenate([pc.astype(BF16) for pc in probs], axis=1), v_h,
                         preferred_element_type=F32)
            alpha_v = jnp.concatenate([alpha] * (DIFF_V_DIM // LANES), axis=1)
            acc_ref[m] = acc_ref[m] * alpha_v + pv

    @pl.when(ki < qi - 1)
    def _():
        step("far")

    @pl.when((ki == qi - 1))
    def _():
        step("prev")

    @pl.when(ki == qi)
    def _():
        step("diag")
        lam_p = lam_ref[...]
        lam = (jnp.exp(jnp.sum(lam_p[0:1] * lam_p[1:2], axis=1, keepdims=True))
               - jnp.exp(jnp.sum(lam_p[2:3] * lam_p[3:4], axis=1, keepdims=True)) + lam_init)
        for h in range(N_DIFF_HEADS):
            inv0 = 1.0 / jnp.sum(l_ref[2 * h], axis=1, keepdims=True)
            inv1 = 1.0 / jnp.sum(l_ref[2 * h + 1], axis=1, keepdims=True)
            o = acc_ref[2 * h] * inv0 - lam * (acc_ref[2 * h + 1] * inv1)
            o = _rms_norm(o, subln_ref[...]) * (1.0 - lam_init)
            o_ref[:, h * DIFF_V_DIM:(h + 1) * DIFF_V_DIM] = o.astype(BF16)


def _diff_attention(proj, rel_bias, lam_params, subln, *, batch, seq, ta, lam_init):
    t = proj.shape[0]
    assert seq % ta == 0 and ta % LANES == 0 and ta >= 2 * LANES
    nq = seq // ta
    pairs = [(q, k) for q in range(nq) for k in range(q + 1)]
    qi = jnp.asarray([q for q, _ in pairs], jnp.int32)
    ki = jnp.asarray([k for _, k in pairs], jnp.int32)
    width = N_DIFF_HEADS * DIFF_V_DIM
    grid_spec = pltpu.PrefetchScalarGridSpec(
        num_scalar_prefetch=2,
        grid=(batch, len(pairs)),
        in_specs=[
            pl.BlockSpec((ta, PROJ_BLOCK), lambda b, p, qi, ki: (b * nq + qi[p], BLK_AQ)),
            pl.BlockSpec((ta, PROJ_BLOCK), lambda b, p, qi, ki: (b * nq + ki[p], BLK_AK)),
            pl.BlockSpec((ta, PROJ_BLOCK), lambda b, p, qi, ki: (b * nq + ki[p], BLK_AV)),
            pl.BlockSpec(memory_space=pltpu.SMEM),
            pl.BlockSpec((4, DIFF_HEAD_DIM), lambda b, p, qi, ki: (0, 0)),
            pl.BlockSpec((1, DIFF_V_DIM), lambda b, p, qi, ki: (0, 0)),
        ],
        out_specs=pl.BlockSpec((ta, width), lambda b, p, qi, ki: (b * nq + qi[p], 0)),
        scratch_shapes=[
            pltpu.VMEM((N_DIFF_MAPS, ta, DIFF_V_DIM), F32),
            pltpu.VMEM((N_DIFF_MAPS, ta, LANES), F32),
            pltpu.VMEM((N_DIFF_MAPS, ta, LANES), F32),
            pltpu.VMEM((2, N_DIFF_MAPS, LANES, LANES), F32),
        ],
    )
    return pl.pallas_call(
        functools.partial(_attn_kernel, ta=ta, lam_init=lam_init),
        grid_spec=grid_spec,
        out_shape=jax.ShapeDtypeStruct((t, width), BF16),
        compiler_params=_params("arbitrary", "arbitrary"),
        name="diff_attention",
    )(qi, ki, proj, proj, proj, rel_bias, lam_params, subln)


def _ret_log_decay(h):
    return math.log(1.0 - 2.0 ** (-5.0 - h))


def _retention_kernel(q_ref, k_ref, v_ref, g_ref, o_ref, state_ref, mask_ref, qdec_ref, kdec_ref, *, chunk):
    b = pl.program_id(0)
    c = pl.program_id(1)

    @pl.when((b == 0) & (c == 0))
    def _():
        rel = (lax.broadcasted_iota(jnp.int32, (chunk, chunk), 0)
               - lax.broadcasted_iota(jnp.int32, (chunk, chunk), 1)).astype(F32)
        idx = lax.broadcasted_iota(jnp.int32, (chunk, RET_KEY_DIM), 0).astype(F32)
        for h in range(N_RET_HEADS):
            log_g = _ret_log_decay(h)
            mask_ref[h] = jnp.where(rel >= 0, jnp.exp(jnp.maximum(rel, 0.0) * log_g), 0.0)
            qdec_ref[h] = jnp.exp((idx + 1.0) * log_g)
            kdec_ref[h] = jnp.exp((chunk - 1.0 - idx) * log_g)

    @pl.when(c == 0)
    def _():
        state_ref[...] = jnp.zeros(state_ref.shape, F32)

    for h in range(N_RET_HEADS):
        sl = slice(h * RET_KEY_DIM, (h + 1) * RET_KEY_DIM)
        q = q_ref[:, sl]
        k = k_ref[:, sl]
        v = v_ref[:, sl]
        scores = lax.dot_general(q, k, (((1,), (1,)), ((), ())), preferred_element_type=F32)
        inner = jnp.dot((scores * mask_ref[h]).astype(BF16), v, preferred_element_type=F32)
        state = state_ref[h]
        cross = jnp.dot(q, state.astype(BF16), preferred_element_type=F32) * qdec_ref[h]
        k_dec = (k.astype(F32) * kdec_ref[h]).astype(BF16)
        state_ref[h] = (state * math.exp(chunk * _ret_log_decay(h))
                        + lax.dot_general(k_dec, v, (((0,), (0,)), ((), ())), preferred_element_type=F32))
        o = inner + cross
        o = o * lax.rsqrt(jnp.mean(o * o, axis=-1, keepdims=True) + EPS)
        o_ref[:, sl] = (o * _silu(g_ref[:, sl])).astype(BF16)


def _retention(proj, gate, *, batch, seq, chunk):
    t = proj.shape[0]
    assert seq % chunk == 0
    nc = seq // chunk
    width = N_RET_HEADS * RET_V_DIM
    return pl.pallas_call(
        functools.partial(_retention_kernel, chunk=chunk),
        grid=(batch, nc),
        in_specs=[
            pl.BlockSpec((chunk, PROJ_BLOCK), lambda b, c: (b * nc + c, BLK_RQ)),
            pl.BlockSpec((chunk, PROJ_BLOCK), lambda b, c: (b * nc + c, BLK_RK)),
            pl.BlockSpec((chunk, PROJ_BLOCK), lambda b, c: (b * nc + c, BLK_RV)),
            pl.BlockSpec((chunk, PROJ_BLOCK), lambda b, c: (b * nc + c, 0)),
        ],
        out_specs=pl.BlockSpec((chunk, width), lambda b, c: (b * nc + c, 0)),
        out_shape=jax.ShapeDtypeStruct((t, width), BF16),
        scratch_shapes=[
            pltpu.VMEM((N_RET_HEADS, RET_KEY_DIM, RET_V_DIM), F32),
            pltpu.VMEM((N_RET_HEADS, chunk, chunk), F32),
            pltpu.VMEM((N_RET_HEADS, chunk, RET_KEY_DIM), F32),
            pltpu.VMEM((N_RET_HEADS, chunk, RET_KEY_DIM), F32),
        ],
        compiler_params=_params("arbitrary", "arbitrary"),
        name="retention",
    )(proj, proj, proj, gate)


def _out_proj_kernel(x_ref, a_ref, r_ref, w_ref, o_ref):
    ka = a_ref.shape[1]
    y = jnp.dot(a_ref[...], w_ref[:ka, :], preferred_element_type=F32)
    y = y + jnp.dot(r_ref[...], w_ref[ka:, :], preferred_element_type=F32)
    o_ref[...] = x_ref[...] + y


def _out_proj(x, a, r, w, layer, *, tm, tn):
    t, d = x.shape
    ka, kr = a.shape[1], r.shape[1]
    assert t % tm == 0 and d % tn == 0 and w.shape[1:] == (ka + kr, d)
    return pl.pallas_call(
        _out_proj_kernel,
        grid=(t // tm, d // tn),
        in_specs=[
            pl.BlockSpec((tm, tn), lambda i, j: (i, j)),
            pl.BlockSpec((tm, ka), lambda i, j: (i, 0)),
            pl.BlockSpec((tm, kr), lambda i, j: (i, 0)),
            pl.BlockSpec((None, ka + kr, tn), lambda i, j: (layer, 0, j)),
        ],
        out_specs=pl.BlockSpec((tm, tn), lambda i, j: (i, j)),
        out_shape=jax.ShapeDtypeStruct((t, d), F32),
        compiler_params=_params("arbitrary", "arbitrary"),
        name="out_proj",
    )(x, a, r, w)


def _pool_kernel(x_ref, xp_ref, g_ref, w_ref, scale_ref, o_ref, h_ref, *, tm, seq):
    i = pl.program_id(0)
    start = (i * tm) % seq
    g = g_ref[...]
    h_prev = _rms_norm(xp_ref[...], g)
    h_ref[0:POOL_HALO, :] = jnp.where(start == 0, 0.0, h_prev)
    h_ref[POOL_HALO:, :] = _rms_norm(x_ref[...], g)
    pos = start + lax.broadcasted_iota(jnp.int32, (tm, 1), 0)
    gd = w_ref.shape[1]
    for gi, win in enumerate(POOL_WINDOWS):
        cols = slice(gi * gd, (gi + 1) * gd)
        total = h_ref[:, cols]
        hg = total[POOL_HALO:]
        span = 1
        while span < win:
            total = total + pltpu.roll(total, span, 0)
            span *= 2
        inv_count = 1.0 / jnp.minimum(pos + 1, win).astype(F32)
        pooled = total[POOL_HALO:] * inv_count - hg
        y = jnp.dot(pooled.astype(BF16), w_ref[gi], preferred_element_type=F32)
        o_ref[:, cols] = x_ref[:, cols] + y * scale_ref[:, cols]


def _pool_mixer(x, g, w, layer, scale, *, seq, tm):
    t, d = x.shape
    assert t % tm == 0 and seq % tm == 0 and tm % POOL_HALO == 0 and max(POOL_WINDOWS) <= POOL_HALO
    assert all(win & (win - 1) == 0 for win in POOL_WINDOWS)
    halo_blocks = tm // POOL_HALO
    return pl.pallas_call(
        functools.partial(_pool_kernel, tm=tm, seq=seq),
        grid=(t // tm,),
        in_specs=[
            pl.BlockSpec((tm, d), lambda i: (i, 0)),
            pl.BlockSpec((POOL_HALO, d), lambda i: (jnp.maximum(i * halo_blocks - 1, 0), 0)),
            pl.BlockSpec((1, d), lambda i: (0, 0)),
            pl.BlockSpec((None,) + w.shape[1:], lambda i: (layer, 0, 0, 0)),
            pl.BlockSpec((1, d), lambda i: (0, 0)),
        ],
        out_specs=pl.BlockSpec((tm, d), lambda i: (i, 0)),
        out_shape=jax.ShapeDtypeStruct((t, d), F32),
        scratch_shapes=[pltpu.VMEM((tm + POOL_HALO, d), F32)],
        compiler_params=_params("arbitrary"),
        name="pool_mixer",
    )(x, x, g, w, scale)


def _conv(u, prev, cw_ref, cb_ref):
    ext = jnp.concatenate([prev, u], axis=0)
    y = cb_ref[...] + u * cw_ref[CONV_WIDTH - 1:CONV_WIDTH, :]
    for tap in range(1, CONV_WIDTH):
        shifted = pltpu.roll(ext, tap, 0)[CONV_HALO:]
        y = y + shifted * cw_ref[CONV_WIDTH - 1 - tap:CONV_WIDTH - tap, :]
    return y


def _gated_chunk(h_ref, w_refs, tail_v_ref, tail_g_ref, f, *, tsub, n_sub, consume):
    wv_ref, wg_ref, cwv_ref, cwg_ref, cbv_ref, cbg_ref = w_refs

    def up(s):
        h_s = h_ref[s * tsub:(s + 1) * tsub, :]
        return (jnp.dot(h_s, wv_ref[...], preferred_element_type=F32),
                jnp.dot(h_s, wg_ref[...], preferred_element_type=F32))

    prev = (tail_v_ref[f], tail_g_ref[f])
    u_cur = up(0)
    for s in range(n_sub):
        u_next = up(s + 1) if s + 1 < n_sub else None
        val = _conv(u_cur[0], prev[0], cwv_ref, cbv_ref)
        gate = _conv(u_cur[1], prev[1], cwg_ref, cbg_ref)
        consume(s, (_silu(gate) * val).astype(BF16))
        prev = (u_cur[0][tsub - CONV_HALO:], u_cur[1][tsub - CONV_HALO:])
        u_cur = u_next
    tail_v_ref[f] = prev[0]
    tail_g_ref[f] = prev[1]


def _ffn_kernel(*refs, tm, seq, final, n_sub):
    if final:
        (x_ref, g_ref, wv_ref, wg_ref, cwv_ref, cwg_ref, cbv_ref, cbg_ref, wd_ref, gf_ref,
         o_ref, h_ref, tail_v_ref, tail_g_ref) = refs
    else:
        (x_ref, g_ref, wv_ref, wg_ref, cwv_ref, cwg_ref, cbv_ref, cbg_ref, wd_ref,
         o_ref, h_ref, tail_v_ref, tail_g_ref) = refs
    i = pl.program_id(0)
    f = pl.program_id(1)
    n_f = pl.num_programs(1)
    tsub = tm // n_sub

    @pl.when(f == 0)
    def _():
        x = x_ref[...]
        h_ref[...] = _rms_norm(x, g_ref[...]).astype(BF16)
        o_ref[...] = x

    @pl.when((i * tm) % seq == 0)
    def _():
        tail_v_ref[f] = jnp.zeros(tail_v_ref.shape[1:], F32)
        tail_g_ref[f] = jnp.zeros(tail_g_ref.shape[1:], F32)

    def consume(s, act):
        o_ref[s * tsub:(s + 1) * tsub, :] += jnp.dot(act, wd_ref[...], preferred_element_type=F32)

    _gated_chunk(h_ref, (wv_ref, wg_ref, cwv_ref, cwg_ref, cbv_ref, cbg_ref), tail_v_ref, tail_g_ref, f,
                 tsub=tsub, n_sub=n_sub, consume=consume)

    if final:
        @pl.when(f == n_f - 1)
        def _():
            o_ref[...] = _rms_norm(o_ref[...], gf_ref[...])


def _conv_ffn(x, g, w_up, conv_w, conv_b, w_down, layer, final_g, *, seq, tm, tf, n_sub):
    t, d = x.shape
    d_ff = w_down.shape[1]
    assert t % tm == 0 and seq % tm == 0 and d_ff % tf == 0 and w_up.shape[1:] == (d, 2 * d_ff)
    n_f = d_ff // tf
    assert tm % n_sub == 0 and (tm // n_sub) % 16 == 0
    final = final_g is not None
    in_specs = [
        pl.BlockSpec((tm, d), lambda i, f: (i, 0)),
        pl.BlockSpec((1, d), lambda i, f: (0, 0)),
        pl.BlockSpec((None, d, tf), lambda i, f: (layer, 0, f)),
        pl.BlockSpec((None, d, tf), lambda i, f: (layer, 0, n_f + f)),
        pl.BlockSpec((None, CONV_WIDTH, tf), lambda i, f: (layer, 0, f)),
        pl.BlockSpec((None, CONV_WIDTH, tf), lambda i, f: (layer, 0, n_f + f)),
        pl.BlockSpec((None, 1, tf), lambda i, f: (layer, 0, f)),
        pl.BlockSpec((None, 1, tf), lambda i, f: (layer, 0, n_f + f)),
        pl.BlockSpec((None, tf, d), lambda i, f: (layer, f, 0)),
    ]
    args = [x, g, w_up, w_up, conv_w, conv_w, conv_b, conv_b, w_down]
    if final:
        in_specs.append(pl.BlockSpec((1, d), lambda i, f: (0, 0)))
        args.append(final_g)
    return pl.pallas_call(
        functools.partial(_ffn_kernel, tm=tm, seq=seq, final=final, n_sub=n_sub),
        grid=(t // tm, n_f),
        in_specs=in_specs,
        out_specs=pl.BlockSpec((tm, d), lambda i, f: (i, 0)),
        out_shape=jax.ShapeDtypeStruct((t, d), F32),
        scratch_shapes=[
            pltpu.VMEM((tm, d), BF16),
            pltpu.VMEM((n_f, CONV_HALO, tf), F32),
            pltpu.VMEM((n_f, CONV_HALO, tf), F32),
        ],
        compiler_params=_params("arbitrary", "arbitrary"),
        name="conv_ffn",
    )(*args)


def _tiles(seq):
    pick = lambda want: min(want, seq)
    return dict(proj_tm=pick(1024), attn_ta=pick(512), ret_chunk=pick(512), out_tm=pick(1024),
                pool_tm=pick(1024), ffn_tm=pick(1024))


def kernel(x, rel_bias, mix_norm, ffn_norm, final_norm, w_in, w_out, lam_q1, lam_k1, lam_q2, lam_k2,
           diff_subln, pool_w, pool_scale, w_up, conv_w, conv_b, w_down):
    batch, seq, d = x.shape
    depth = mix_norm.shape[0]
    tiles = _tiles(seq)
    xf = x.reshape(batch * seq, d)

    pos = jnp.arange(seq, dtype=F32)
    inv_freq = 1.0 / (ROPE_BASE ** jnp.linspace(0.0, 1.0, RET_KEY_DIM // 2, dtype=F32))
    ang = pos[:, None] * inv_freq[None, :]
    cos, sin = jnp.cos(ang), jnp.sin(ang)

    w_in_b, w_out_b, pool_w_b = w_in.astype(BF16), w_out.astype(BF16), pool_w.astype(BF16)
    w_up_b, w_down_b = w_up.astype(BF16), w_down.astype(BF16)
    conv_b3 = conv_b[:, None, :]

    for l in range(depth):
        g_mix = mix_norm[l][None, :]
        if l % 2 == 0:
            i = l // 2
            lam_init = 0.8 - 0.6 * math.exp(-0.3 * l)
            proj, gate = _in_proj(xf, g_mix, w_in_b, i, cos, sin, seq=seq, tm=tiles["proj_tm"],
                                  n_sub=1 if l == 0 else 2)
            lam_params = jnp.stack([lam_q1[i], lam_k1[i], lam_q2[i], lam_k2[i]]).astype(F32)
            a_out = _diff_attention(proj, rel_bias, lam_params, diff_subln[i][None, :], batch=batch, seq=seq,
                                    ta=tiles["attn_ta"], lam_init=lam_init)
            r_out = _retention(proj, gate, batch=batch, seq=seq, chunk=tiles["ret_chunk"])
            xf = _out_proj(xf, a_out, r_out, w_out_b, i, tm=tiles["out_tm"], tn=PROJ_BLOCK)
        else:
            j = l // 2
            xf = _pool_mixer(xf, g_mix, pool_w_b, j, pool_scale[j][None, :], seq=seq, tm=tiles["pool_tm"])
        final_g = final_norm[None, :] if l == depth - 1 else None
        xf = _conv_ffn(xf, ffn_norm[l][None, :], w_up_b, conv_w, conv_b3, w_down_b, l, final_g, seq=seq,
                       tm=tiles["ffn_tm"], tf=512, n_sub=4)
    return xf.reshape(batch, seq, d)
```

```python
import functools
import math

import jax
import jax.numpy as jnp
from jax import lax
from jax.experimental import pallas as pl
from jax.experimental.pallas import tpu as pltpu

F32 = jnp.float32
BF16 = jnp.bfloat16

N_DIFF_HEADS = 4
DIFF_HEAD_DIM = 128
DIFF_V_DIM = 2 * DIFF_HEAD_DIM
N_DIFF_MAPS = 2 * N_DIFF_HEADS
N_RET_HEADS = 4
RET_KEY_DIM = 256
RET_V_DIM = 256
ROPE_BASE = 10000.0
NUM_BUCKETS = 32
MAX_DISTANCE = 128
POOL_WINDOWS = (2, 4, 8, 16)
POOL_HALO = 16
CONV_WIDTH = 3
CONV_HALO = 8
EPS = 1e-6
NEG_INF = -1e30
LOG2_E = math.log2(math.e)
PROJ_BLOCK = 1024
(BLK_AQ, BLK_AK, BLK_AV, BLK_RQ, BLK_RK, BLK_RV, BLK_RG) = range(7)

LANES = 128
VMEM_LIMIT = 56 * 1024 * 1024


def _params(*semantics):
    return pltpu.CompilerParams(dimension_semantics=semantics, vmem_limit_bytes=VMEM_LIMIT)


def _rms_norm(xf, g):
    y = xf * lax.rsqrt(jnp.mean(xf * xf, axis=-1, keepdims=True) + EPS)
    return y * g


def _silu(z):
    half = 0.5 * z
    return half + half * jnp.tanh(half)


def _in_proj_kernel(x_ref, g_ref, w_ref, cos_ref, sin_ref, proj_ref, gate_ref, h_ref, *, n_sub):
    j = pl.program_id(1)
    tsub = h_ref.shape[0] // n_sub

    @pl.when(j == 0)
    def _():
        h_ref[...] = _rms_norm(x_ref[...], g_ref[...]).astype(BF16)

    def matmul(s):
        return jnp.dot(h_ref[s * tsub:(s + 1) * tsub, :], w_ref[...], preferred_element_type=F32)

    def plain(scale):
        def write(rows, acc):
            proj_ref[rows, :] = (acc if scale is None else acc * scale).astype(BF16)
        return write

    def rotary(scale):
        def write(rows, acc):
            cos = cos_ref[rows, :]
            sin = sin_ref[rows, :]
            half = RET_KEY_DIM // 2
            for h in range(N_RET_HEADS):
                lo = h * RET_KEY_DIM
                t1 = acc[:, lo:lo + half]
                t2 = acc[:, lo + half:lo + RET_KEY_DIM]
                proj_ref[rows, lo:lo + half] = ((t1 * cos - t2 * sin) * scale).astype(BF16)
                proj_ref[rows, lo + half:lo + RET_KEY_DIM] = ((t1 * sin + t2 * cos) * scale).astype(BF16)
        return write

    def gate(rows, acc):
        gate_ref[rows, :] = acc

    acc0 = matmul(0)

    def finish(write):
        acc = acc0
        for s in range(n_sub):
            acc_next = matmul(s + 1) if s + 1 < n_sub else None
            write(slice(s * tsub, (s + 1) * tsub), acc)
            acc = acc_next

    @pl.when(j == BLK_AQ)
    def _():
        finish(plain(DIFF_HEAD_DIM ** -0.5 * LOG2_E))

    @pl.when((j == BLK_AK) | (j == BLK_AV) | (j == BLK_RV))
    def _():
        finish(plain(None))

    @pl.when(j == BLK_RQ)
    def _():
        finish(rotary(1.0))

    @pl.when(j == BLK_RK)
    def _():
        finish(rotary(RET_KEY_DIM ** -0.5))

    @pl.when(j == BLK_RG)
    def _():
        finish(gate)


def _in_proj(x, g, w, layer, cos, sin, *, seq, tm, n_sub):
    t, d = x.shape
    n_blk = w.shape[2] // PROJ_BLOCK
    assert n_blk == 7 and t % tm == 0 and seq % tm == 0 and tm % (16 * n_sub) == 0
    tiles_per_seq = seq // tm
    return pl.pallas_call(
        functools.partial(_in_proj_kernel, n_sub=n_sub),
        grid=(t // tm, n_blk),
        in_specs=[
            pl.BlockSpec((tm, d), lambda i, j: (i, 0)),
            pl.BlockSpec((1, d), lambda i, j: (0, 0)),
            pl.BlockSpec((None, d, PROJ_BLOCK), lambda i, j: (layer, 0, j)),
            pl.BlockSpec((tm, RET_KEY_DIM // 2), lambda i, j: (i % tiles_per_seq, 0)),
            pl.BlockSpec((tm, RET_KEY_DIM // 2), lambda i, j: (i % tiles_per_seq, 0)),
        ],
        out_specs=[
            pl.BlockSpec((tm, PROJ_BLOCK), lambda i, j: (i, jnp.minimum(j, BLK_RV))),
            pl.BlockSpec((tm, PROJ_BLOCK), lambda i, j: (i, 0)),
        ],
        out_shape=[
            jax.ShapeDtypeStruct((t, BLK_RG * PROJ_BLOCK), BF16),
            jax.ShapeDtypeStruct((t, PROJ_BLOCK), F32),
        ],
        scratch_shapes=[pltpu.VMEM((tm, d), BF16)],
        compiler_params=_params("arbitrary", "arbitrary"),
        name="in_proj",
    )(x, g, w, cos, sin)


def _t5_bucket(dist):
    max_exact = NUM_BUCKETS // 2
    is_small = dist < max_exact
    df = jnp.maximum(dist, 1).astype(F32)
    large = max_exact + (jnp.log(df / max_exact) / math.log(MAX_DISTANCE / max_exact)
                         * (NUM_BUCKETS - max_exact)).astype(jnp.int32)
    large = jnp.minimum(large, NUM_BUCKETS - 1)
    return jnp.where(is_small, dist, large)


def _attn_kernel(qi_ref, ki_ref, q_ref, k_ref, v_ref, bias_ref, lam_ref, subln_ref, o_ref,
                 acc_ref, m_ref, l_ref, band_ref, *, ta, lam_init):
    b = pl.program_id(0)
    p = pl.program_id(1)
    qi = qi_ref[p]
    ki = ki_ref[p]
    n_sub = ta // LANES

    @pl.when((b == 0) & (p == 0))
    def _():
        row = lax.broadcasted_iota(jnp.int32, (LANES, LANES), 0)
        col = lax.broadcasted_iota(jnp.int32, (LANES, LANES), 1)
        for slot in range(2):
            bucket = _t5_bucket(jnp.maximum(row - col + slot * LANES, 0))
            for m in range(N_DIFF_MAPS):
                val = jnp.zeros((LANES, LANES), F32)
                for bk in range(NUM_BUCKETS):
                    val = jnp.where(bucket == bk, bias_ref[bk, m], val)
                band_ref[slot, m] = (val - bias_ref[NUM_BUCKETS - 1, m]) * LOG2_E

    @pl.when(ki == 0)
    def _():
        m_ref[...] = jnp.full(m_ref.shape, NEG_INF, F32)
        l_ref[...] = jnp.zeros(l_ref.shape, F32)
        acc_ref[...] = jnp.zeros(acc_ref.shape, F32)

    def biased_slabs(s, m, kind):
        slabs = []
        for c in range(n_sub):
            sc = s[:, c * LANES:(c + 1) * LANES]
            if kind == "far" or (kind == "prev" and c != n_sub - 1):
                slabs.append(sc)
                continue
            pieces = []
            for a in range(n_sub):
                blk = sc[a * LANES:(a + 1) * LANES]
                if kind == "prev":
                    if a == 0:
                        blk = blk + band_ref[1, m]
                elif a < c:
                    blk = jnp.full((LANES, LANES), NEG_INF, F32)
                elif a == c:
                    row = lax.broadcasted_iota(jnp.int32, (LANES, LANES), 0)
                    col = lax.broadcasted_iota(jnp.int32, (LANES, LANES), 1)
                    blk = jnp.where(row >= col, blk + band_ref[0, m], NEG_INF)
                elif a == c + 1:
                    blk = blk + band_ref[1, m]
                pieces.append(blk)
            slabs.append(jnp.concatenate(pieces, axis=0))
        return slabs

    def scores(m):
        q_m = q_ref[:, m * DIFF_HEAD_DIM:(m + 1) * DIFF_HEAD_DIM]
        k_m = k_ref[:, m * DIFF_HEAD_DIM:(m + 1) * DIFF_HEAD_DIM]
        return lax.dot_general(q_m, k_m, (((1,), (1,)), ((), ())), preferred_element_type=F32)

    def step(kind):
        for m in range(N_DIFF_MAPS):
            s = scores(m)
            h = m // 2
            v_h = v_ref[:, h * DIFF_V_DIM:(h + 1) * DIFF_V_DIM]
            slabs = biased_slabs(s, m, kind)
            part_max = slabs[0]
            for sc in slabs[1:]:
                part_max = jnp.maximum(part_max, sc)
            m_prev = m_ref[m]
            m_new = jnp.maximum(m_prev, jnp.max(part_max, axis=1, keepdims=True))
            alpha = jnp.exp2(m_prev - m_new)
            probs = [jnp.exp2(sc - m_new) for sc in slabs]
            part_sum = probs[0]
            for pc in probs[1:]:
                part_sum = part_sum + pc
            m_ref[m] = m_new
            l_ref[m] = alpha * l_ref[m] + part_sum
            pv = jnp.dot(jnp.concatenate([pc.astype(BF16) for pc in probs], axis=1), v_h,
                         preferred_element_type=F32)
            alpha_v = jnp.concatenate([alpha] * (DIFF_V_DIM // LANES), axis=1)
            acc_ref[m] = acc_ref[m] * alpha_v + pv

    @pl.when(ki < qi - 1)
    def _():
        step("far")

    @pl.when((ki == qi - 1))
    def _():
        step("prev")

    @pl.when(ki == qi)
    def _():
        step("diag")
        lam_p = lam_ref[...]
        lam = (jnp.exp(jnp.sum(lam_p[0:1] * lam_p[1:2], axis=1, keepdims=True))
               - jnp.exp(jnp.sum(lam_p[2:3] * lam_p[3:4], axis=1, keepdims=True)) + lam_init)
        for h in range(N_DIFF_HEADS):
            inv0 = 1.0 / jnp.sum(l_ref[2 * h], axis=1, keepdims=True)
            inv1 = 1.0 / jnp.sum(l_ref[2 * h + 1], axis=1, keepdims=True)
            o = acc_ref[2 * h] * inv0 - lam * (acc_ref[2 * h + 1] * inv1)
            o = _rms_norm(o, subln_ref[...]) * (1.0 - lam_init)
            o_ref[:, h * DIFF_V_DIM:(h + 1) * DIFF_V_DIM] = o.astype(BF16)


def _diff_attention(proj, rel_bias, lam_params, subln, *, batch, seq, ta, lam_init):
    t = proj.shape[0]
    assert seq % ta == 0 and ta % LANES == 0 and ta >= 2 * LANES
    nq = seq // ta
    pairs = [(q, k) for q in range(nq) for k in range(q + 1)]
    qi = jnp.asarray([q for q, _ in pairs], jnp.int32)
    ki = jnp.asarray([k for _, k in pairs], jnp.int32)
    width = N_DIFF_HEADS * DIFF_V_DIM
    grid_spec = pltpu.PrefetchScalarGridSpec(
        num_scalar_prefetch=2,
        grid=(batch, len(pairs)),
        in_specs=[
            pl.BlockSpec((ta, PROJ_BLOCK), lambda b, p, qi, ki: (b * nq + qi[p], BLK_AQ)),
            pl.BlockSpec((ta, PROJ_BLOCK), lambda b, p, qi, ki: (b * nq + ki[p], BLK_AK)),
            pl.BlockSpec((ta, PROJ_BLOCK), lambda b, p, qi, ki: (b * nq + ki[p], BLK_AV)),
            pl.BlockSpec(memory_space=pltpu.SMEM),
            pl.BlockSpec((4, DIFF_HEAD_DIM), lambda b, p, qi, ki: (0, 0)),
            pl.BlockSpec((1, DIFF_V_DIM), lambda b, p, qi, ki: (0, 0)),
        ],
        out_specs=pl.BlockSpec((ta, width), lambda b, p, qi, ki: (b * nq + qi[p], 0)),
        scratch_shapes=[
            pltpu.VMEM((N_DIFF_MAPS, ta, DIFF_V_DIM), F32),
            pltpu.VMEM((N_DIFF_MAPS, ta, LANES), F32),
            pltpu.VMEM((N_DIFF_MAPS, ta, LANES), F32),
            pltpu.VMEM((2, N_DIFF_MAPS, LANES, LANES), F32),
        ],
    )
    return pl.pallas_call(
        functools.partial(_attn_kernel, ta=ta, lam_init=lam_init),
        grid_spec=grid_spec,
        out_shape=jax.ShapeDtypeStruct((t, width), BF16),
        compiler_params=_params("arbitrary", "arbitrary"),
        name="diff_attention",
    )(qi, ki, proj, proj, proj, rel_bias, lam_params, subln)


def _ret_log_decay(h):
    return math.log(1.0 - 2.0 ** (-5.0 - h))


def _retention_kernel(q_ref, k_ref, v_ref, g_ref, o_ref, state_ref, mask_ref, qdec_ref, kdec_ref, *, chunk):
    b = pl.program_id(0)
    c = pl.program_id(1)

    @pl.when((b == 0) & (c == 0))
    def _():
        rel = (lax.broadcasted_iota(jnp.int32, (chunk, chunk), 0)
               - lax.broadcasted_iota(jnp.int32, (chunk, chunk), 1)).astype(F32)
        idx = lax.broadcasted_iota(jnp.int32, (chunk, RET_KEY_DIM), 0).astype(F32)
        for h in range(N_RET_HEADS):
            log_g = _ret_log_decay(h)
            mask_ref[h] = jnp.where(rel >= 0, jnp.exp(jnp.maximum(rel, 0.0) * log_g), 0.0)
            qdec_ref[h] = jnp.exp((idx + 1.0) * log_g)
            kdec_ref[h] = jnp.exp((chunk - 1.0 - idx) * log_g)

    @pl.when(c == 0)
    def _():
        state_ref[...] = jnp.zeros(state_ref.shape, F32)

    for h in range(N_RET_HEADS):
        sl = slice(h * RET_KEY_DIM, (h + 1) * RET_KEY_DIM)
        q = q_ref[:, sl]
        k = k_ref[:, sl]
        v = v_ref[:, sl]
        scores = lax.dot_general(q, k, (((1,), (1,)), ((), ())), preferred_element_type=F32)
        inner = jnp.dot((scores * mask_ref[h]).astype(BF16), v, preferred_element_type=F32)
        state = state_ref[h]
        cross = jnp.dot(q, state.astype(BF16), preferred_element_type=F32) * qdec_ref[h]
        k_dec = (k.astype(F32) * kdec_ref[h]).astype(BF16)
        state_ref[h] = (state * math.exp(chunk * _ret_log_decay(h))
                        + lax.dot_general(k_dec, v, (((0,), (0,)), ((), ())), preferred_element_type=F32))
        o = inner + cross
        o = o * lax.rsqrt(jnp.mean(o * o, axis=-1, keepdims=True) + EPS)
        o_ref[:, sl] = (o * _silu(g_ref[:, sl])).astype(BF16)


def _retention(proj, gate, *, batch, seq, chunk):
    t = proj.shape[0]
    assert seq % chunk == 0
    nc = seq // chunk
    width = N_RET_HEADS * RET_V_DIM
    return pl.pallas_call(
        functools.partial(_retention_kernel, chunk=chunk),
        grid=(batch, nc),
        in_specs=[
            pl.BlockSpec((chunk, PROJ_BLOCK), lambda b, c: (b * nc + c, BLK_RQ)),
            pl.BlockSpec((chunk, PROJ_BLOCK), lambda b, c: (b * nc + c, BLK_RK)),
            pl.BlockSpec((chunk, PROJ_BLOCK), lambda b, c: (b * nc + c, BLK_RV)),
            pl.BlockSpec((chunk, PROJ_BLOCK), lambda b, c: (b * nc + c, 0)),
        ],
        out_specs=pl.BlockSpec((chunk, width), lambda b, c: (b * nc + c, 0)),
        out_shape=jax.ShapeDtypeStruct((t, width), BF16),
        scratch_shapes=[
            pltpu.VMEM((N_RET_HEADS, RET_KEY_DIM, RET_V_DIM), F32),
            pltpu.VMEM((N_RET_HEADS, chunk, chunk), F32),
            pltpu.VMEM((N_RET_HEADS, chunk, RET_KEY_DIM), F32),
            pltpu.VMEM((N_RET_HEADS, chunk, RET_KEY_DIM), F32),
        ],
        compiler_params=_params("arbitrary", "arbitrary"),
        name="retention",
    )(proj, proj, proj, gate)


def _out_proj_kernel(x_ref, a_ref, r_ref, w_ref, o_ref):
    ka = a_ref.shape[1]
    y = jnp.dot(a_ref[...], w_ref[:ka, :], preferred_element_type=F32)
    y = y + jnp.dot(r_ref[...], w_ref[ka:, :], preferred_element_type=F32)
    o_ref[...] = x_ref[...] + y


def _out_proj(x, a, r, w, layer, *, tm, tn):
    t, d = x.shape
    ka, kr = a.shape[1], r.shape[1]
    assert t % tm == 0 and d % tn == 0 and w.shape[1:] == (ka + kr, d)
    return pl.pallas_call(
        _out_proj_kernel,
        grid=(t // tm, d // tn),
        in_specs=[
            pl.BlockSpec((tm, tn), lambda i, j: (i, j)),
            pl.BlockSpec((tm, ka), lambda i, j: (i, 0)),
            pl.BlockSpec((tm, kr), lambda i, j: (i, 0)),
            pl.BlockSpec((None, ka + kr, tn), lambda i, j: (layer, 0, j)),
        ],
        out_specs=pl.BlockSpec((tm, tn), lambda i, j: (i, j)),
        out_shape=jax.ShapeDtypeStruct((t, d), F32),
        compiler_params=_params("arbitrary", "arbitrary"),
        name="out_proj",
    )(x, a, r, w)


def _pool_kernel(x_ref, xp_ref, g_ref, w_ref, scale_ref, o_ref, h_ref, *, tm, seq):
    i = pl.program_id(0)
    start = (i * tm) % seq
    g = g_ref[...]
    h_prev = _rms_norm(xp_ref[...], g)
    h_ref[0:POOL_HALO, :] = jnp.where(start == 0, 0.0, h_prev)
    h_ref[POOL_HALO:, :] = _rms_norm(x_ref[...], g)
    pos = start + lax.broadcasted_iota(jnp.int32, (tm, 1), 0)
    gd = w_ref.shape[1]
    for gi, win in enumerate(POOL_WINDOWS):
        cols = slice(gi * gd, (gi + 1) * gd)
        total = h_ref[:, cols]
        hg = total[POOL_HALO:]
        span = 1
        while span < win:
            total = total + pltpu.roll(total, span, 0)
            span *= 2
        inv_count = 1.0 / jnp.minimum(pos + 1, win).astype(F32)
        pooled = total[POOL_HALO:] * inv_count - hg
        y = jnp.dot(pooled.astype(BF16), w_ref[gi], preferred_element_type=F32)
        o_ref[:, cols] = x_ref[:, cols] + y * scale_ref[:, cols]


def _pool_mixer(x, g, w, layer, scale, *, seq, tm):
    t, d = x.shape
    assert t % tm == 0 and seq % tm == 0 and tm % POOL_HALO == 0 and max(POOL_WINDOWS) <= POOL_HALO
    assert all(win & (win - 1) == 0 for win in POOL_WINDOWS)
    halo_blocks = tm // POOL_HALO
    return pl.pallas_call(
        functools.partial(_pool_kernel, tm=tm, seq=seq),
        grid=(t // tm,),
        in_specs=[
            pl.BlockSpec((tm, d), lambda i: (i, 0)),
            pl.BlockSpec((POOL_HALO, d), lambda i: (jnp.maximum(i * halo_blocks - 1, 0), 0)),
            pl.BlockSpec((1, d), lambda i: (0, 0)),
            pl.BlockSpec((None,) + w.shape[1:], lambda i: (layer, 0, 0, 0)),
            pl.BlockSpec((1, d), lambda i: (0, 0)),
        ],
        out_specs=pl.BlockSpec((tm, d), lambda i: (i, 0)),
        out_shape=jax.ShapeDtypeStruct((t, d), F32),
        scratch_shapes=[pltpu.VMEM((tm + POOL_HALO, d), F32)],
        compiler_params=_params("arbitrary"),
        name="pool_mixer",
    )(x, x, g, w, scale)


def _conv(u, prev, cw_ref, cb_ref):
    ext = jnp.concatenate([prev, u], axis=0)
    y = cb_ref[...] + u * cw_ref[CONV_WIDTH - 1:CONV_WIDTH, :]
    for tap in range(1, CONV_WIDTH):
        shifted = pltpu.roll(ext, tap, 0)[CONV_HALO:]
        y = y + shifted * cw_ref[CONV_WIDTH - 1 - tap:CONV_WIDTH - tap, :]
    return y


def _gated_chunk(h_ref, w_refs, tail_v_ref, tail_g_ref, f, *, tsub, n_sub, consume):
    wv_ref, wg_ref, cwv_ref, cwg_ref, cbv_ref, cbg_ref = w_refs

    def up(s):
        h_s = h_ref[s * tsub:(s + 1) * tsub, :]
        return (jnp.dot(h_s, wv_ref[...], preferred_element_type=F32),
                jnp.dot(h_s, wg_ref[...], preferred_element_type=F32))

    prev = (tail_v_ref[f], tail_g_ref[f])
    u_cur = up(0)
    for s in range(n_sub):
        u_next = up(s + 1) if s + 1 < n_sub else None
        val = _conv(u_cur[0], prev[0], cwv_ref, cbv_ref)
        gate = _conv(u_cur[1], prev[1], cwg_ref, cbg_ref)
        consume(s, (_silu(gate) * val).astype(BF16))
        prev = (u_cur[0][tsub - CONV_HALO:], u_cur[1][tsub - CONV_HALO:])
        u_cur = u_next
    tail_v_ref[f] = prev[0]
    tail_g_ref[f] = prev[1]


def _ffn_kernel(*refs, tm, seq, final, n_sub):
    if final:
        (x_ref, g_ref, wv_ref, wg_ref, cwv_ref, cwg_ref, cbv_ref, cbg_ref, wd_ref, gf_ref,
         o_ref, h_ref, tail_v_ref, tail_g_ref) = refs
    else:
        (x_ref, g_ref, wv_ref, wg_ref, cwv_ref, cwg_ref, cbv_ref, cbg_ref, wd_ref,
         o_ref, h_ref, tail_v_ref, tail_g_ref) = refs
    i = pl.program_id(0)
    f = pl.program_id(1)
    n_f = pl.num_programs(1)
    tsub = tm // n_sub

    @pl.when(f == 0)
    def _():
        x = x_ref[...]
        h_ref[...] = _rms_norm(x, g_ref[...]).astype(BF16)
        o_ref[...] = x

    @pl.when((i * tm) % seq == 0)
    def _():
        tail_v_ref[f] = jnp.zeros(tail_v_ref.shape[1:], F32)
        tail_g_ref[f] = jnp.zeros(tail_g_ref.shape[1:], F32)

    def consume(s, act):
        o_ref[s * tsub:(s + 1) * tsub, :] += jnp.dot(act, wd_ref[...], preferred_element_type=F32)

    _gated_chunk(h_ref, (wv_ref, wg_ref, cwv_ref, cwg_ref, cbv_ref, cbg_ref), tail_v_ref, tail_g_ref, f,
                 tsub=tsub, n_sub=n_sub, consume=consume)

    if final:
        @pl.when(f == n_f - 1)
        def _():
            o_ref[...] = _rms_norm(o_ref[...], gf_ref[...])


def _conv_ffn(x, g, w_up, conv_w, conv_b, w_down, layer, final_g, *, seq, tm, tf, n_sub):
    t, d = x.shape
    d_ff = w_down.shape[1]
    assert t % tm == 0 and seq % tm == 0 and d_ff % tf == 0 and w_up.shape[1:] == (d, 2 * d_ff)
    n_f = d_ff // tf
    assert tm % n_sub == 0 and (tm // n_sub) % 16 == 0
    final = final_g is not None
    in_specs = [
        pl.BlockSpec((tm, d), lambda i, f: (i, 0)),
        pl.BlockSpec((1, d), lambda i, f: (0, 0)),
        pl.BlockSpec((None, d, tf), lambda i, f: (layer, 0, f)),
        pl.BlockSpec((None, d, tf), lambda i, f: (layer, 0, n_f + f)),
        pl.BlockSpec((None, CONV_WIDTH, tf), lambda i, f: (layer, 0, f)),
        pl.BlockSpec((None, CONV_WIDTH, tf), lambda i, f: (layer, 0, n_f + f)),
        pl.BlockSpec((None, 1, tf), lambda i, f: (layer, 0, f)),
        pl.BlockSpec((None, 1, tf), lambda i, f: (layer, 0, n_f + f)),
        pl.BlockSpec((None, tf, d), lambda i, f: (layer, f, 0)),
    ]
    args = [x, g, w_up, w_up, conv_w, conv_w, conv_b, conv_b, w_down]
    if final:
        in_specs.append(pl.BlockSpec((1, d), lambda i, f: (0, 0)))
        args.append(final_g)
    return pl.pallas_call(
        functools.partial(_ffn_kernel, tm=tm, seq=seq, final=final, n_sub=n_sub),
        grid=(t // tm, n_f),
        in_specs=in_specs,
        out_specs=pl.BlockSpec((tm, d), lambda i, f: (i, 0)),
        out_shape=jax.ShapeDtypeStruct((t, d), F32),
        scratch_shapes=[
            pltpu.VMEM((tm, d), BF16),
            pltpu.VMEM((n_f, CONV_HALO, tf), F32),
            pltpu.VMEM((n_f, CONV_HALO, tf), F32),
        ],
        compiler_params=_params("arbitrary", "arbitrary"),
        name="conv_ffn",
    )(*args)


def _tiles(seq):
    pick = lambda want: min(want, seq)
    return dict(proj_tm=pick(1024), attn_ta=pick(512), ret_chunk=pick(512), out_tm=pick(1024),
                pool_tm=pick(1024), ffn_tm=pick(1024))


def kernel(x, rel_bias, mix_norm, ffn_norm, final_norm, w_in, w_out, lam_q1, lam_k1, lam_q2, lam_k2,
           diff_subln, pool_w, pool_scale, w_up, conv_w, conv_b, w_down):
    batch, seq, d = x.shape
    depth = mix_norm.shape[0]
    tiles = _tiles(seq)
    xf = x.reshape(batch * seq, d)

    pos = jnp.arange(seq, dtype=F32)
    inv_freq = 1.0 / (ROPE_BASE ** jnp.linspace(0.0, 1.0, RET_KEY_DIM // 2, dtype=F32))
    ang = pos[:, None] * inv_freq[None, :]
    cos, sin = jnp.cos(ang), jnp.sin(ang)

    w_in_b, w_out_b, pool_w_b = w_in.astype(BF16), w_out.astype(BF16), pool_w.astype(BF16)
    w_up_b, w_down_b = w_up.astype(BF16), w_down.astype(BF16)
    conv_b3 = conv_b[:, None, :]

    for l in range(depth):
        g_mix = mix_norm[l][None, :]
        if l % 2 == 0:
            i = l // 2
            lam_init = 0.8 - 0.6 * math.exp(-0.3 * l)
            proj, gate = _in_proj(xf, g_mix, w_in_b, i, cos, sin, seq=seq, tm=tiles["proj_tm"],
                                  n_sub=1)
            lam_params = jnp.stack([lam_q1[i], lam_k1[i], lam_q2[i], lam_k2[i]]).astype(F32)
            a_out = _diff_attention(proj, rel_bias, lam_params, diff_subln[i][None, :], batch=batch, seq=seq,
                                    ta=tiles["attn_ta"], lam_init=lam_init)
            r_out = _retention(proj, gate, batch=batch, seq=seq, chunk=tiles["ret_chunk"])
            if l == 0:
                xf = _out_proj(xf, a_out, r_out, w_out_b, i, tm=tiles["out_tm"], tn=PROJ_BLOCK)
            else:
                xf = _out_proj(xf, a_out, r_out, w_out_b, i, tm=min(512, seq), tn=d)
        else:
            j = l // 2
            xf = _pool_mixer(xf, g_mix, pool_w_b, j, pool_scale[j][None, :], seq=seq, tm=tiles["pool_tm"])
        final_g = final_norm[None, :] if l == depth - 1 else None
        xf = _conv_ffn(xf, ffn_norm[l][None, :], w_up_b, conv_w, conv_b3, w_down_b, l, final_g, seq=seq,
                       tm=tiles["ffn_tm"], tf=512, n_sub=(4, 2, 1, 4)[l])
    return xf.reshape(batch, seq, d)
```

```python
import functools
import math

import jax
import jax.numpy as jnp
from jax import lax
from jax.experimental import pallas as pl
from jax.experimental.pallas import tpu as pltpu

F32 = jnp.float32
BF16 = jnp.bfloat16

N_DIFF_HEADS = 4
DIFF_HEAD_DIM = 128
DIFF_V_DIM = 2 * DIFF_HEAD_DIM
N_DIFF_MAPS = 2 * N_DIFF_HEADS
N_RET_HEADS = 4
RET_KEY_DIM = 256
RET_V_DIM = 256
ROPE_BASE = 10000.0
NUM_BUCKETS = 32
MAX_DISTANCE = 128
POOL_WINDOWS = (2, 4, 8, 16)
POOL_HALO = 16
CONV_WIDTH = 3
CONV_HALO = 8
EPS = 1e-6
NEG_INF = -1e30
LOG2_E = math.log2(math.e)
PROJ_BLOCK = 1024
(BLK_AQ, BLK_AK, BLK_AV, BLK_RQ, BLK_RK, BLK_RV, BLK_RG) = range(7)

LANES = 128
VMEM_LIMIT = 56 * 1024 * 1024


def _params(*semantics):
    return pltpu.CompilerParams(dimension_semantics=semantics, vmem_limit_bytes=VMEM_LIMIT)


def _rms_norm(xf, g):
    y = xf * lax.rsqrt(jnp.mean(xf * xf, axis=-1, keepdims=True) + EPS)
    return y * g


def _silu(z):
    half = 0.5 * z
    return half + half * jnp.tanh(half)


def _in_proj_kernel(x_ref, g_ref, w_ref, cos_ref, sin_ref, proj_ref, gate_ref, h_ref, *, n_sub):
    j = pl.program_id(1)
    csub = PROJ_BLOCK // n_sub

    @pl.when(j == 0)
    def _():
        h_ref[...] = _rms_norm(x_ref[...], g_ref[...]).astype(BF16)

    def matmul(s):
        return jnp.dot(h_ref[...], w_ref[:, s * csub:(s + 1) * csub], preferred_element_type=F32)

    def plain(scale):
        def write(c0, acc):
            proj_ref[:, c0:c0 + csub] = (acc if scale is None else acc * scale).astype(BF16)
        return write

    def rotary(scale):
        def write(c0, acc):
            cos = cos_ref[...]
            sin = sin_ref[...]
            half = RET_KEY_DIM // 2
            for lo in range(0, csub, RET_KEY_DIM):
                t1 = acc[:, lo:lo + half]
                t2 = acc[:, lo + half:lo + RET_KEY_DIM]
                proj_ref[:, c0 + lo:c0 + lo + half] = ((t1 * cos - t2 * sin) * scale).astype(BF16)
                proj_ref[:, c0 + lo + half:c0 + lo + RET_KEY_DIM] = ((t1 * sin + t2 * cos) * scale).astype(BF16)
        return write

    def gate(c0, acc):
        gate_ref[:, c0:c0 + csub] = acc

    acc0 = matmul(0)

    def finish(write):
        acc = acc0
        for s in range(n_sub):
            acc_next = matmul(s + 1) if s + 1 < n_sub else None
            write(s * csub, acc)
            acc = acc_next

    @pl.when(j == BLK_AQ)
    def _():
        finish(plain(DIFF_HEAD_DIM ** -0.5 * LOG2_E))

    @pl.when((j == BLK_AK) | (j == BLK_AV) | (j == BLK_RV))
    def _():
        finish(plain(None))

    @pl.when(j == BLK_RQ)
    def _():
        finish(rotary(1.0))

    @pl.when(j == BLK_RK)
    def _():
        finish(rotary(RET_KEY_DIM ** -0.5))

    @pl.when(j == BLK_RG)
    def _():
        finish(gate)


def _in_proj(x, g, w, layer, cos, sin, *, seq, tm, n_sub):
    t, d = x.shape
    n_blk = w.shape[2] // PROJ_BLOCK
    assert n_blk == 7 and t % tm == 0 and seq % tm == 0 and PROJ_BLOCK % (RET_KEY_DIM * n_sub) == 0
    tiles_per_seq = seq // tm
    return pl.pallas_call(
        functools.partial(_in_proj_kernel, n_sub=n_sub),
        grid=(t // tm, n_blk),
        in_specs=[
            pl.BlockSpec((tm, d), lambda i, j: (i, 0)),
            pl.BlockSpec((1, d), lambda i, j: (0, 0)),
            pl.BlockSpec((None, d, PROJ_BLOCK), lambda i, j: (layer, 0, j)),
            pl.BlockSpec((tm, RET_KEY_DIM // 2), lambda i, j: (i % tiles_per_seq, 0)),
            pl.BlockSpec((tm, RET_KEY_DIM // 2), lambda i, j: (i % tiles_per_seq, 0)),
        ],
        out_specs=[
            pl.BlockSpec((tm, PROJ_BLOCK), lambda i, j: (i, jnp.minimum(j, BLK_RV))),
            pl.BlockSpec((tm, PROJ_BLOCK), lambda i, j: (i, 0)),
        ],
        out_shape=[
            jax.ShapeDtypeStruct((t, BLK_RG * PROJ_BLOCK), BF16),
            jax.ShapeDtypeStruct((t, PROJ_BLOCK), F32),
        ],
        scratch_shapes=[pltpu.VMEM((tm, d), BF16)],
        compiler_params=_params("arbitrary", "arbitrary"),
        name="in_proj",
    )(x, g, w, cos, sin)


def _t5_bucket(dist):
    max_exact = NUM_BUCKETS // 2
    is_small = dist < max_exact
    df = jnp.maximum(dist, 1).astype(F32)
    large = max_exact + (jnp.log(df / max_exact) / math.log(MAX_DISTANCE / max_exact)
                         * (NUM_BUCKETS - max_exact)).astype(jnp.int32)
    large = jnp.minimum(large, NUM_BUCKETS - 1)
    return jnp.where(is_small, dist, large)


def _attn_kernel(qi_ref, ki_ref, q_ref, k_ref, v_ref, bias_ref, lam_ref, subln_ref, o_ref,
                 acc_ref, m_ref, l_ref, band_ref, *, ta, lam_init):
    b = pl.program_id(0)
    p = pl.program_id(1)
    qi = qi_ref[p]
    ki = ki_ref[p]
    n_sub = ta // LANES

    @pl.when((b == 0) & (p == 0))
    def _():
        row = lax.broadcasted_iota(jnp.int32, (LANES, LANES), 0)
        col = lax.broadcasted_iota(jnp.int32, (LANES, LANES), 1)
        for slot in range(2):
            bucket = _t5_bucket(jnp.maximum(row - col + slot * LANES, 0))
            for m in range(N_DIFF_MAPS):
                val = jnp.zeros((LANES, LANES), F32)
                for bk in range(NUM_BUCKETS):
                    val = jnp.where(bucket == bk, bias_ref[bk, m], val)
                band_ref[slot, m] = (val - bias_ref[NUM_BUCKETS - 1, m]) * LOG2_E

    @pl.when(ki == 0)
    def _():
        m_ref[...] = jnp.full(m_ref.shape, NEG_INF, F32)
        l_ref[...] = jnp.zeros(l_ref.shape, F32)
        acc_ref[...] = jnp.zeros(acc_ref.shape, F32)

    def biased_slabs(s, m, kind):
        slabs = []
        for c in range(n_sub):
            sc = s[:, c * LANES:(c + 1) * LANES]
            if kind == "far" or (kind == "prev" and c != n_sub - 1):
                slabs.append(sc)
                continue
            pieces = []
            for a in range(n_sub):
                blk = sc[a * LANES:(a + 1) * LANES]
                if kind == "prev":
                    if a == 0:
                        blk = blk + band_ref[1, m]
                elif a < c:
                    blk = jnp.full((LANES, LANES), NEG_INF, F32)
                elif a == c:
                    row = lax.broadcasted_iota(jnp.int32, (LANES, LANES), 0)
                    col = lax.broadcasted_iota(jnp.int32, (LANES, LANES), 1)
                    blk = jnp.where(row >= col, blk + band_ref[0, m], NEG_INF)
                elif a == c + 1:
                    blk = blk + band_ref[1, m]
                pieces.append(blk)
            slabs.append(jnp.concatenate(pieces, axis=0))
        return slabs

    def scores(m):
        q_m = q_ref[:, m * DIFF_HEAD_DIM:(m + 1) * DIFF_HEAD_DIM]
        k_m = k_ref[:, m * DIFF_HEAD_DIM:(m + 1) * DIFF_HEAD_DIM]
        return lax.dot_general(q_m, k_m, (((1,), (1,)), ((), ())), preferred_element_type=F32)

    def step(kind):
        for m in range(N_DIFF_MAPS):
            s = scores(m)
            h = m // 2
            v_h = v_ref[:, h * DIFF_V_DIM:(h + 1) * DIFF_V_DIM]
            slabs = biased_slabs(s, m, kind)
            part_max = slabs[0]
            for sc in slabs[1:]:
                part_max = jnp.maximum(part_max, sc)
            m_prev = m_ref[m]
            m_new = jnp.maximum(m_prev, jnp.max(part_max, axis=1, keepdims=True))
            alpha = jnp.exp2(m_prev - m_new)
            probs = [jnp.exp2(sc - m_new) for sc in slabs]
            part_sum = probs[0]
            for pc in probs[1:]:
                part_sum = part_sum + pc
            m_ref[m] = m_new
            l_ref[m] = alpha * l_ref[m] + part_sum
            pv = jnp.dot(jnp.concatenate([pc.astype(BF16) for pc in probs], axis=1), v_h,
                         preferred_element_type=F32)
            alpha_v = jnp.concatenate([alpha] * (DIFF_V_DIM // LANES), axis=1)
            acc_ref[m] = acc_ref[m] * alpha_v + pv

    @pl.when(ki < qi - 1)
    def _():
        step("far")

    @pl.when((ki == qi - 1))
    def _():
        step("prev")

    @pl.when(ki == qi)
    def _():
        step("diag")
        lam_p = lam_ref[...]
        lam = (jnp.exp(jnp.sum(lam_p[0:1] * lam_p[1:2], axis=1, keepdims=True))
               - jnp.exp(jnp.sum(lam_p[2:3] * lam_p[3:4], axis=1, keepdims=True)) + lam_init)
        for h in range(N_DIFF_HEADS):
            inv0 = 1.0 / jnp.sum(l_ref[2 * h], axis=1, keepdims=True)
            inv1 = 1.0 / jnp.sum(l_ref[2 * h + 1], axis=1, keepdims=True)
            o = acc_ref[2 * h] * inv0 - lam * (acc_ref[2 * h + 1] * inv1)
            o = _rms_norm(o, subln_ref[...]) * (1.0 - lam_init)
            o_ref[:, h * DIFF_V_DIM:(h + 1) * DIFF_V_DIM] = o.astype(BF16)


def _diff_attention(proj, rel_bias, lam_params, subln, *, batch, seq, ta, lam_init):
    t = proj.shape[0]
    assert seq % ta == 0 and ta % LANES == 0 and ta >= 2 * LANES
    nq = seq // ta
    pairs = [(q, k) for q in range(nq) for k in range(q + 1)]
    qi = jnp.asarray([q for q, _ in pairs], jnp.int32)
    ki = jnp.asarray([k for _, k in pairs], jnp.int32)
    width = N_DIFF_HEADS * DIFF_V_DIM
    grid_spec = pltpu.PrefetchScalarGridSpec(
        num_scalar_prefetch=2,
        grid=(batch, len(pairs)),
        in_specs=[
            pl.BlockSpec((ta, PROJ_BLOCK), lambda b, p, qi, ki: (b * nq + qi[p], BLK_AQ)),
            pl.BlockSpec((ta, PROJ_BLOCK), lambda b, p, qi, ki: (b * nq + ki[p], BLK_AK)),
            pl.BlockSpec((ta, PROJ_BLOCK), lambda b, p, qi, ki: (b * nq + ki[p], BLK_AV)),
            pl.BlockSpec(memory_space=pltpu.SMEM),
            pl.BlockSpec((4, DIFF_HEAD_DIM), lambda b, p, qi, ki: (0, 0)),
            pl.BlockSpec((1, DIFF_V_DIM), lambda b, p, qi, ki: (0, 0)),
        ],
        out_specs=pl.BlockSpec((ta, width), lambda b, p, qi, ki: (b * nq + qi[p], 0)),
        scratch_shapes=[
            pltpu.VMEM((N_DIFF_MAPS, ta, DIFF_V_DIM), F32),
            pltpu.VMEM((N_DIFF_MAPS, ta, LANES), F32),
            pltpu.VMEM((N_DIFF_MAPS, ta, LANES), F32),
            pltpu.VMEM((2, N_DIFF_MAPS, LANES, LANES), F32),
        ],
    )
    return pl.pallas_call(
        functools.partial(_attn_kernel, ta=ta, lam_init=lam_init),
        grid_spec=grid_spec,
        out_shape=jax.ShapeDtypeStruct((t, width), BF16),
        compiler_params=_params("arbitrary", "arbitrary"),
        name="diff_attention",
    )(qi, ki, proj, proj, proj, rel_bias, lam_params, subln)


def _ret_log_decay(h):
    return math.log(1.0 - 2.0 ** (-5.0 - h))


def _retention_kernel(q_ref, k_ref, v_ref, g_ref, o_ref, state_ref, mask_ref, qdec_ref, kdec_ref, *, chunk):
    b = pl.program_id(0)
    c = pl.program_id(1)

    @pl.when((b == 0) & (c == 0))
    def _():
        rel = (lax.broadcasted_iota(jnp.int32, (chunk, chunk), 0)
               - lax.broadcasted_iota(jnp.int32, (chunk, chunk), 1)).astype(F32)
        idx = lax.broadcasted_iota(jnp.int32, (chunk, RET_KEY_DIM), 0).astype(F32)
        for h in range(N_RET_HEADS):
            log_g = _ret_log_decay(h)
            mask_ref[h] = jnp.where(rel >= 0, jnp.exp(jnp.maximum(rel, 0.0) * log_g), 0.0)
            qdec_ref[h] = jnp.exp((idx + 1.0) * log_g)
            kdec_ref[h] = jnp.exp((chunk - 1.0 - idx) * log_g)

    @pl.when(c == 0)
    def _():
        state_ref[...] = jnp.zeros(state_ref.shape, F32)

    for h in range(N_RET_HEADS):
        sl = slice(h * RET_KEY_DIM, (h + 1) * RET_KEY_DIM)
        q = q_ref[:, sl]
        k = k_ref[:, sl]
        v = v_ref[:, sl]
        scores = lax.dot_general(q, k, (((1,), (1,)), ((), ())), preferred_element_type=F32)
        inner = jnp.dot((scores * mask_ref[h]).astype(BF16), v, preferred_element_type=F32)
        state = state_ref[h]
        cross = jnp.dot(q, state.astype(BF16), preferred_element_type=F32) * qdec_ref[h]
        k_dec = (k.astype(F32) * kdec_ref[h]).astype(BF16)
        state_ref[h] = (state * math.exp(chunk * _ret_log_decay(h))
                        + lax.dot_general(k_dec, v, (((0,), (0,)), ((), ())), preferred_element_type=F32))
        o = inner + cross
        o = o * lax.rsqrt(jnp.mean(o * o, axis=-1, keepdims=True) + EPS)
        o_ref[:, sl] = (o * _silu(g_ref[:, sl])).astype(BF16)


def _retention(proj, gate, *, batch, seq, chunk):
    t = proj.shape[0]
    assert seq % chunk == 0
    nc = seq // chunk
    width = N_RET_HEADS * RET_V_DIM
    return pl.pallas_call(
        functools.partial(_retention_kernel, chunk=chunk),
        grid=(batch, nc),
        in_specs=[
            pl.BlockSpec((chunk, PROJ_BLOCK), lambda b, c: (b * nc + c, BLK_RQ)),
            pl.BlockSpec((chunk, PROJ_BLOCK), lambda b, c: (b * nc + c, BLK_RK)),
            pl.BlockSpec((chunk, PROJ_BLOCK), lambda b, c: (b * nc + c, BLK_RV)),
            pl.BlockSpec((chunk, PROJ_BLOCK), lambda b, c: (b * nc + c, 0)),
        ],
        out_specs=pl.BlockSpec((chunk, width), lambda b, c: (b * nc + c, 0)),
        out_shape=jax.ShapeDtypeStruct((t, width), BF16),
        scratch_shapes=[
            pltpu.VMEM((N_RET_HEADS, RET_KEY_DIM, RET_V_DIM), F32),
            pltpu.VMEM((N_RET_HEADS, chunk, chunk), F32),
            pltpu.VMEM((N_RET_HEADS, chunk, RET_KEY_DIM), F32),
            pltpu.VMEM((N_RET_HEADS, chunk, RET_KEY_DIM), F32),
        ],
        compiler_params=_params("arbitrary", "arbitrary"),
        name="retention",
    )(proj, proj, proj, gate)


def _out_proj_kernel(x_ref, a_ref, r_ref, w_ref, o_ref):
    ka = a_ref.shape[1]
    y = jnp.dot(a_ref[...], w_ref[:ka, :], preferred_element_type=F32)
    y = y + jnp.dot(r_ref[...], w_ref[ka:, :], preferred_element_type=F32)
    o_ref[...] = x_ref[...] + y


def _out_proj(x, a, r, w, layer, *, tm, tn):
    t, d = x.shape
    ka, kr = a.shape[1], r.shape[1]
    assert t % tm == 0 and d % tn == 0 and w.shape[1:] == (ka + kr, d)
    w_mode = pl.Buffered(1) if tn == d else None
    return pl.pallas_call(
        _out_proj_kernel,
        grid=(t // tm, d // tn),
        in_specs=[
            pl.BlockSpec((tm, tn), lambda i, j: (i, j)),
            pl.BlockSpec((tm, ka), lambda i, j: (i, 0)),
            pl.BlockSpec((tm, kr), lambda i, j: (i, 0)),
            pl.BlockSpec((None, ka + kr, tn), lambda i, j: (layer, 0, j), pipeline_mode=w_mode),
        ],
        out_specs=pl.BlockSpec((tm, tn), lambda i, j: (i, j)),
        out_shape=jax.ShapeDtypeStruct((t, d), F32),
        compiler_params=_params("arbitrary", "arbitrary"),
        name="out_proj",
    )(x, a, r, w)


def _pool_kernel(x_ref, xp_ref, g_ref, w_ref, scale_ref, o_ref, h_ref, *, tm, seq):
    i = pl.program_id(0)
    start = (i * tm) % seq
    g = g_ref[...]
    h_prev = _rms_norm(xp_ref[...], g)
    h_ref[0:POOL_HALO, :] = jnp.where(start == 0, 0.0, h_prev)
    h_ref[POOL_HALO:, :] = _rms_norm(x_ref[...], g)
    pos = start + lax.broadcasted_iota(jnp.int32, (tm, 1), 0)
    gd = w_ref.shape[1]
    for gi, win in enumerate(POOL_WINDOWS):
        cols = slice(gi * gd, (gi + 1) * gd)
        total = h_ref[:, cols]
        hg = total[POOL_HALO:]
        span = 1
        while span < win:
            total = total + pltpu.roll(total, span, 0)
            span *= 2
        inv_count = 1.0 / jnp.minimum(pos + 1, win).astype(F32)
        pooled = total[POOL_HALO:] * inv_count - hg
        y = jnp.dot(pooled.astype(BF16), w_ref[gi], preferred_element_type=F32)
        o_ref[:, cols] = x_ref[:, cols] + y * scale_ref[:, cols]


def _pool_mixer(x, g, w, layer, scale, *, seq, tm):
    t, d = x.shape
    assert t % tm == 0 and seq % tm == 0 and tm % POOL_HALO == 0 and max(POOL_WINDOWS) <= POOL_HALO
    assert all(win & (win - 1) == 0 for win in POOL_WINDOWS)
    halo_blocks = tm // POOL_HALO
    return pl.pallas_call(
        functools.partial(_pool_kernel, tm=tm, seq=seq),
        grid=(t // tm,),
        in_specs=[
            pl.BlockSpec((tm, d), lambda i: (i, 0)),
            pl.BlockSpec((POOL_HALO, d), lambda i: (jnp.maximum(i * halo_blocks - 1, 0), 0)),
            pl.BlockSpec((1, d), lambda i: (0, 0)),
            pl.BlockSpec((None,) + w.shape[1:], lambda i: (layer, 0, 0, 0)),
            pl.BlockSpec((1, d), lambda i: (0, 0)),
        ],
        out_specs=pl.BlockSpec((tm, d), lambda i: (i, 0)),
        out_shape=jax.ShapeDtypeStruct((t, d), F32),
        scratch_shapes=[pltpu.VMEM((tm + POOL_HALO, d), F32)],
        compiler_params=_params("arbitrary"),
        name="pool_mixer",
    )(x, x, g, w, scale)


def _conv(u, prev, cw_ref, cb_ref):
    ext = jnp.concatenate([prev, u], axis=0)
    y = cb_ref[...] + u * cw_ref[CONV_WIDTH - 1:CONV_WIDTH, :]
    for tap in range(1, CONV_WIDTH):
        shifted = pltpu.roll(ext, tap, 0)[CONV_HALO:]
        y = y + shifted * cw_ref[CONV_WIDTH - 1 - tap:CONV_WIDTH - tap, :]
    return y


def _gated_chunk(h_ref, w_refs, tail_v_ref, tail_g_ref, f, *, tsub, n_sub, consume):
    wv_ref, wg_ref, cwv_ref, cwg_ref, cbv_ref, cbg_ref = w_refs

    def up(s):
        h_s = h_ref[s * tsub:(s + 1) * tsub, :]
        return (jnp.dot(h_s, wv_ref[...], preferred_element_type=F32),
                jnp.dot(h_s, wg_ref[...], preferred_element_type=F32))

    prev = (tail_v_ref[f], tail_g_ref[f])
    u_cur = up(0)
    for s in range(n_sub):
        u_next = up(s + 1) if s + 1 < n_sub else None
        val = _conv(u_cur[0], prev[0], cwv_ref, cbv_ref)
        gate = _conv(u_cur[1], prev[1], cwg_ref, cbg_ref)
        consume(s, (_silu(gate) * val).astype(BF16))
        prev = (u_cur[0][tsub - CONV_HALO:], u_cur[1][tsub - CONV_HALO:])
        u_cur = u_next
    tail_v_ref[f] = prev[0]
    tail_g_ref[f] = prev[1]


def _ffn_kernel(*refs, tm, seq, final, n_sub):
    if final:
        (x_ref, g_ref, wv_ref, wg_ref, cwv_ref, cwg_ref, cbv_ref, cbg_ref, wd_ref, gf_ref,
         o_ref, h_ref, tail_v_ref, tail_g_ref) = refs
    else:
        (x_ref, g_ref, wv_ref, wg_ref, cwv_ref, cwg_ref, cbv_ref, cbg_ref, wd_ref,
         o_ref, h_ref, tail_v_ref, tail_g_ref) = refs
    i = pl.program_id(0)
    f = pl.program_id(1)
    n_f = pl.num_programs(1)
    tsub = tm // n_sub

    @pl.when(f == 0)
    def _():
        x = x_ref[...]
        h_ref[...] = _rms_norm(x, g_ref[...]).astype(BF16)
        o_ref[...] = x

    @pl.when((i * tm) % seq == 0)
    def _():
        tail_v_ref[f] = jnp.zeros(tail_v_ref.shape[1:], F32)
        tail_g_ref[f] = jnp.zeros(tail_g_ref.shape[1:], F32)

    def consume(s, act):
        o_ref[s * tsub:(s + 1) * tsub, :] += jnp.dot(act, wd_ref[...], preferred_element_type=F32)

    _gated_chunk(h_ref, (wv_ref, wg_ref, cwv_ref, cwg_ref, cbv_ref, cbg_ref), tail_v_ref, tail_g_ref, f,
                 tsub=tsub, n_sub=n_sub, consume=consume)

    if final:
        @pl.when(f == n_f - 1)
        def _():
            o_ref[...] = _rms_norm(o_ref[...], gf_ref[...])


def _conv_ffn(x, g, w_up, conv_w, conv_b, w_down, layer, final_g, *, seq, tm, tf, n_sub):
    t, d = x.shape
    d_ff = w_down.shape[1]
    assert t % tm == 0 and seq % tm == 0 and d_ff % tf == 0 and w_up.shape[1:] == (d, 2 * d_ff)
    n_f = d_ff // tf
    assert tm % n_sub == 0 and (tm // n_sub) % 16 == 0
    final = final_g is not None
    in_specs = [
        pl.BlockSpec((tm, d), lambda i, f: (i, 0)),
        pl.BlockSpec((1, d), lambda i, f: (0, 0)),
        pl.BlockSpec((None, d, tf), lambda i, f: (layer, 0, f)),
        pl.BlockSpec((None, d, tf), lambda i, f: (layer, 0, n_f + f)),
        pl.BlockSpec((None, CONV_WIDTH, tf), lambda i, f: (layer, 0, f)),
        pl.BlockSpec((None, CONV_WIDTH, tf), lambda i, f: (layer, 0, n_f + f)),
        pl.BlockSpec((None, 1, tf), lambda i, f: (layer, 0, f)),
        pl.BlockSpec((None, 1, tf), lambda i, f: (layer, 0, n_f + f)),
        pl.BlockSpec((None, tf, d), lambda i, f: (layer, f, 0)),
    ]
    args = [x, g, w_up, w_up, conv_w, conv_w, conv_b, conv_b, w_down]
    if final:
        in_specs.append(pl.BlockSpec((1, d), lambda i, f: (0, 0)))
        args.append(final_g)
    return pl.pallas_call(
        functools.partial(_ffn_kernel, tm=tm, seq=seq, final=final, n_sub=n_sub),
        grid=(t // tm, n_f),
        in_specs=in_specs,
        out_specs=pl.BlockSpec((tm, d), lambda i, f: (i, 0)),
        out_shape=jax.ShapeDtypeStruct((t, d), F32),
        scratch_shapes=[
            pltpu.VMEM((tm, d), BF16),
            pltpu.VMEM((n_f, CONV_HALO, tf), F32),
            pltpu.VMEM((n_f, CONV_HALO, tf), F32),
        ],
        compiler_params=_params("arbitrary", "arbitrary"),
        name="conv_ffn",
    )(*args)


def _tiles(seq):
    pick = lambda want: min(want, seq)
    return dict(proj_tm=pick(1024), attn_ta=pick(512), ret_chunk=pick(512), out_tm=pick(512),
                pool_tm=pick(1024), ffn_tm=pick(1024))


def kernel(x, rel_bias, mix_norm, ffn_norm, final_norm, w_in, w_out, lam_q1, lam_k1, lam_q2, lam_k2,
           diff_subln, pool_w, pool_scale, w_up, conv_w, conv_b, w_down):
    batch, seq, d = x.shape
    depth = mix_norm.shape[0]
    tiles = _tiles(seq)
    xf = x.reshape(batch * seq, d)

    pos = jnp.arange(seq, dtype=F32)
    inv_freq = 1.0 / (ROPE_BASE ** jnp.linspace(0.0, 1.0, RET_KEY_DIM // 2, dtype=F32))
    ang = pos[:, None] * inv_freq[None, :]
    cos, sin = jnp.cos(ang), jnp.sin(ang)

    w_in_b, w_out_b, pool_w_b = w_in.astype(BF16), w_out.astype(BF16), pool_w.astype(BF16)
    w_up_b, w_down_b = w_up.astype(BF16), w_down.astype(BF16)
    conv_b3 = conv_b[:, None, :]

    for l in range(depth):
        g_mix = mix_norm[l][None, :]
        if l % 2 == 0:
            i = l // 2
            lam_init = 0.8 - 0.6 * math.exp(-0.3 * l)
            proj, gate = _in_proj(xf, g_mix, w_in_b, i, cos, sin, seq=seq, tm=tiles["proj_tm"],
                                  n_sub=1 if l == 0 else 2)
            lam_params = jnp.stack([lam_q1[i], lam_k1[i], lam_q2[i], lam_k2[i]]).astype(F32)
            a_out = _diff_attention(proj, rel_bias, lam_params, diff_subln[i][None, :], batch=batch, seq=seq,
                                    ta=tiles["attn_ta"], lam_init=lam_init)
            r_out = _retention(proj, gate, batch=batch, seq=seq, chunk=tiles["ret_chunk"])
            xf = _out_proj(xf, a_out, r_out, w_out_b, i, tm=tiles["out_tm"], tn=d)
        else:
            j = l // 2
            xf = _pool_mixer(xf, g_mix, pool_w_b, j, pool_scale[j][None, :], seq=seq, tm=tiles["pool_tm"])
        final_g = final_norm[None, :] if l == depth - 1 else None
        xf = _conv_ffn(xf, ffn_norm[l][None, :], w_up_b, conv_w, conv_b3, w_down_b, l, final_g, seq=seq,
                       tm=tiles["ffn_tm"], tf=512, n_sub=2)
    return xf.reshape(batch, seq, d)
```

```python
import functools
import math

import jax
import jax.numpy as jnp
from jax import lax
from jax.experimental import pallas as pl
from jax.experimental.pallas import tpu as pltpu

F32 = jnp.float32
BF16 = jnp.bfloat16

N_DIFF_HEADS = 4
DIFF_HEAD_DIM = 128
DIFF_V_DIM = 2 * DIFF_HEAD_DIM
N_DIFF_MAPS = 2 * N_DIFF_HEADS
N_RET_HEADS = 4
RET_KEY_DIM = 256
RET_V_DIM = 256
ROPE_BASE = 10000.0
NUM_BUCKETS = 32
MAX_DISTANCE = 128
POOL_WINDOWS = (2, 4, 8, 16)
POOL_HALO = 16
CONV_WIDTH = 3
CONV_HALO = 8
EPS = 1e-6
NEG_INF = -1e30
LOG2_E = math.log2(math.e)
PROJ_BLOCK = 1024
FFN_SUB_TILES = 2
CAST_WIDTH = 1024
(BLK_AQ, BLK_AK, BLK_AV, BLK_RQ, BLK_RK, BLK_RV, BLK_RG) = range(7)

LANES = 128
VMEM_LIMIT = 56 * 1024 * 1024


def _params(*semantics):
    return pltpu.CompilerParams(dimension_semantics=semantics, vmem_limit_bytes=VMEM_LIMIT)


def _rms_norm(xf, g):
    y = xf * lax.rsqrt(jnp.mean(xf * xf, axis=-1, keepdims=True) + EPS)
    return y * g


def _silu(z):
    half = 0.5 * z
    return half + half * jnp.tanh(half)


def _in_proj_kernel(x_ref, g_ref, w_ref, cos_ref, sin_ref, proj_ref, gate_ref, h_ref):
    j = pl.program_id(1)

    @pl.when(j == 0)
    def _():
        h_ref[...] = _rms_norm(x_ref[...], g_ref[...]).astype(BF16)

    acc = jnp.dot(h_ref[...], w_ref[...], preferred_element_type=F32)

    def rotary(scale):
        cos = cos_ref[...]
        sin = sin_ref[...]
        half = RET_KEY_DIM // 2
        for lo in range(0, PROJ_BLOCK, RET_KEY_DIM):
            t1 = acc[:, lo:lo + half]
            t2 = acc[:, lo + half:lo + RET_KEY_DIM]
            proj_ref[:, lo:lo + half] = ((t1 * cos - t2 * sin) * scale).astype(BF16)
            proj_ref[:, lo + half:lo + RET_KEY_DIM] = ((t1 * sin + t2 * cos) * scale).astype(BF16)

    @pl.when(j == BLK_AQ)
    def _():
        proj_ref[...] = (acc * (DIFF_HEAD_DIM ** -0.5 * LOG2_E)).astype(BF16)

    @pl.when((j == BLK_AK) | (j == BLK_AV) | (j == BLK_RV))
    def _():
        proj_ref[...] = acc.astype(BF16)

    @pl.when(j == BLK_RQ)
    def _():
        rotary(1.0)

    @pl.when(j == BLK_RK)
    def _():
        rotary(RET_KEY_DIM ** -0.5)

    @pl.when(j == BLK_RG)
    def _():
        gate_ref[...] = acc


def _in_proj(x, g, w, layer, cos, sin, *, seq, tm):
    t, d = x.shape
    n_blk = w.shape[2] // PROJ_BLOCK
    assert n_blk == 7 and t % tm == 0 and seq % tm == 0
    tiles_per_seq = seq // tm
    return pl.pallas_call(
        _in_proj_kernel,
        grid=(t // tm, n_blk),
        in_specs=[
            pl.BlockSpec((tm, d), lambda i, j: (i, 0)),
            pl.BlockSpec((1, d), lambda i, j: (0, 0)),
            pl.BlockSpec((None, d, PROJ_BLOCK), lambda i, j: (layer, 0, j)),
            pl.BlockSpec((tm, RET_KEY_DIM // 2), lambda i, j: (i % tiles_per_seq, 0)),
            pl.BlockSpec((tm, RET_KEY_DIM // 2), lambda i, j: (i % tiles_per_seq, 0)),
        ],
        out_specs=[
            pl.BlockSpec((tm, PROJ_BLOCK), lambda i, j: (i, jnp.minimum(j, BLK_RV))),
            pl.BlockSpec((tm, PROJ_BLOCK), lambda i, j: (i, 0)),
        ],
        out_shape=[
            jax.ShapeDtypeStruct((t, BLK_RG * PROJ_BLOCK), BF16),
            jax.ShapeDtypeStruct((t, PROJ_BLOCK), F32),
        ],
        scratch_shapes=[pltpu.VMEM((tm, d), BF16)],
        compiler_params=_params("arbitrary", "arbitrary"),
        name="in_proj",
    )(x, g, w, cos, sin)


def _t5_bucket(dist):
    max_exact = NUM_BUCKETS // 2
    is_small = dist < max_exact
    df = jnp.maximum(dist, 1).astype(F32)
    large = max_exact + (jnp.log(df / max_exact) / math.log(MAX_DISTANCE / max_exact)
                         * (NUM_BUCKETS - max_exact)).astype(jnp.int32)
    large = jnp.minimum(large, NUM_BUCKETS - 1)
    return jnp.where(is_small, dist, large)


def _attn_kernel(qi_ref, ki_ref, q_ref, k_ref, v_ref, bias_ref, lam_ref, subln_ref, o_ref,
                 acc_ref, m_ref, l_ref, band_ref, *, ta, lam_init):
    b = pl.program_id(0)
    p = pl.program_id(1)
    qi = qi_ref[p]
    ki = ki_ref[p]
    n_sub = ta // LANES

    @pl.when((b == 0) & (p == 0))
    def _():
        row = lax.broadcasted_iota(jnp.int32, (LANES, LANES), 0)
        col = lax.broadcasted_iota(jnp.int32, (LANES, LANES), 1)
        for slot in range(2):
            bucket = _t5_bucket(jnp.maximum(row - col + slot * LANES, 0))
            for m in range(N_DIFF_MAPS):
                val = jnp.zeros((LANES, LANES), F32)
                for bk in range(NUM_BUCKETS):
                    val = jnp.where(bucket == bk, bias_ref[bk, m], val)
                band_ref[slot, m] = (val - bias_ref[NUM_BUCKETS - 1, m]) * LOG2_E

    @pl.when(ki == 0)
    def _():
        m_ref[...] = jnp.full(m_ref.shape, NEG_INF, F32)
        l_ref[...] = jnp.zeros(l_ref.shape, F32)
        acc_ref[...] = jnp.zeros(acc_ref.shape, F32)

    def biased_slabs(s, m, kind):
        slabs = []
        for c in range(n_sub):
            sc = s[:, c * LANES:(c + 1) * LANES]
            if kind == "far" or (kind == "prev" and c != n_sub - 1):
                slabs.append(sc)
                continue
            pieces = []
            for a in range(n_sub):
                blk = sc[a * LANES:(a + 1) * LANES]
                if kind == "prev":
                    if a == 0:
                        blk = blk + band_ref[1, m]
                elif a < c:
                    blk = jnp.full((LANES, LANES), NEG_INF, F32)
                elif a == c:
                    row = lax.broadcasted_iota(jnp.int32, (LANES, LANES), 0)
                    col = lax.broadcasted_iota(jnp.int32, (LANES, LANES), 1)
                    blk = jnp.where(row >= col, blk + band_ref[0, m], NEG_INF)
                elif a == c + 1:
                    blk = blk + band_ref[1, m]
                pieces.append(blk)
            slabs.append(jnp.concatenate(pieces, axis=0))
        return slabs

    def scores(m):
        q_m = q_ref[:, m * DIFF_HEAD_DIM:(m + 1) * DIFF_HEAD_DIM]
        k_m = k_ref[:, m * DIFF_HEAD_DIM:(m + 1) * DIFF_HEAD_DIM]
        return lax.dot_general(q_m, k_m, (((1,), (1,)), ((), ())), preferred_element_type=F32)

    def step(kind):
        for m in range(N_DIFF_MAPS):
            s = scores(m)
            h = m // 2
            v_h = v_ref[:, h * DIFF_V_DIM:(h + 1) * DIFF_V_DIM]
            slabs = biased_slabs(s, m, kind)
            part_max = slabs[0]
            for sc in slabs[1:]:
                part_max = jnp.maximum(part_max, sc)
            m_prev = m_ref[m]
            m_new = jnp.maximum(m_prev, jnp.max(part_max, axis=1, keepdims=True))
            alpha = jnp.exp2(m_prev - m_new)
            probs = [jnp.exp2(sc - m_new) for sc in slabs]
            part_sum = probs[0]
            for pc in probs[1:]:
                part_sum = part_sum + pc
            m_ref[m] = m_new
            l_ref[m] = alpha * l_ref[m] + part_sum
            pv = jnp.dot(jnp.concatenate([pc.astype(BF16) for pc in probs], axis=1), v_h,
                         preferred_element_type=F32)
            alpha_v = jnp.concatenate([alpha] * (DIFF_V_DIM // LANES), axis=1)
            acc_ref[m] = acc_ref[m] * alpha_v + pv

    @pl.when(ki < qi - 1)
    def _():
        step("far")

    @pl.when((ki == qi - 1))
    def _():
        step("prev")

    @pl.when(ki == qi)
    def _():
        step("diag")
        lam_p = lam_ref[...]
        lam = (jnp.exp(jnp.sum(lam_p[0:1] * lam_p[1:2], axis=1, keepdims=True))
               - jnp.exp(jnp.sum(lam_p[2:3] * lam_p[3:4], axis=1, keepdims=True)) + lam_init)
        for h in range(N_DIFF_HEADS):
            inv0 = 1.0 / jnp.sum(l_ref[2 * h], axis=1, keepdims=True)
            inv1 = 1.0 / jnp.sum(l_ref[2 * h + 1], axis=1, keepdims=True)
            o = acc_ref[2 * h] * inv0 - lam * (acc_ref[2 * h + 1] * inv1)
            o = _rms_norm(o, subln_ref[...]) * (1.0 - lam_init)
            o_ref[:, h * DIFF_V_DIM:(h + 1) * DIFF_V_DIM] = o.astype(BF16)


def _diff_attention(proj, rel_bias, lam_params, subln, *, batch, seq, ta, lam_init):
    t = proj.shape[0]
    assert seq % ta == 0 and ta % LANES == 0 and ta >= 2 * LANES
    nq = seq // ta
    pairs = [(q, k) for q in range(nq) for k in range(q + 1)]
    qi = jnp.asarray([q for q, _ in pairs], jnp.int32)
    ki = jnp.asarray([k for _, k in pairs], jnp.int32)
    width = N_DIFF_HEADS * DIFF_V_DIM
    grid_spec = pltpu.PrefetchScalarGridSpec(
        num_scalar_prefetch=2,
        grid=(batch, len(pairs)),
        in_specs=[
            pl.BlockSpec((ta, PROJ_BLOCK), lambda b, p, qi, ki: (b * nq + qi[p], BLK_AQ)),
            pl.BlockSpec((ta, PROJ_BLOCK), lambda b, p, qi, ki: (b * nq + ki[p], BLK_AK)),
            pl.BlockSpec((ta, PROJ_BLOCK), lambda b, p, qi, ki: (b * nq + ki[p], BLK_AV)),
            pl.BlockSpec(memory_space=pltpu.SMEM),
            pl.BlockSpec((4, DIFF_HEAD_DIM), lambda b, p, qi, ki: (0, 0)),
            pl.BlockSpec((1, DIFF_V_DIM), lambda b, p, qi, ki: (0, 0)),
        ],
        out_specs=pl.BlockSpec((ta, width), lambda b, p, qi, ki: (b * nq + qi[p], 0)),
        scratch_shapes=[
            pltpu.VMEM((N_DIFF_MAPS, ta, DIFF_V_DIM), F32),
            pltpu.VMEM((N_DIFF_MAPS, ta, LANES), F32),
            pltpu.VMEM((N_DIFF_MAPS, ta, LANES), F32),
            pltpu.VMEM((2, N_DIFF_MAPS, LANES, LANES), F32),
        ],
    )
    return pl.pallas_call(
        functools.partial(_attn_kernel, ta=ta, lam_init=lam_init),
        grid_spec=grid_spec,
        out_shape=jax.ShapeDtypeStruct((t, width), BF16),
        compiler_params=_params("arbitrary", "arbitrary"),
        name="diff_attention",
    )(qi, ki, proj, proj, proj, rel_bias, lam_params, subln)


def _ret_log_decay(h):
    return math.log(1.0 - 2.0 ** (-5.0 - h))


def _retention_kernel(q_ref, k_ref, v_ref, g_ref, o_ref, state_ref, mask_ref, qdec_ref, kdec_ref, *, chunk):
    b = pl.program_id(0)
    c = pl.program_id(1)

    @pl.when((b == 0) & (c == 0))
    def _():
        rel = (lax.broadcasted_iota(jnp.int32, (chunk, chunk), 0)
               - lax.broadcasted_iota(jnp.int32, (chunk, chunk), 1)).astype(F32)
        idx = lax.broadcasted_iota(jnp.int32, (chunk, RET_KEY_DIM), 0).astype(F32)
        for h in range(N_RET_HEADS):
            log_g = _ret_log_decay(h)
            mask_ref[h] = jnp.where(rel >= 0, jnp.exp(jnp.maximum(rel, 0.0) * log_g), 0.0)
            qdec_ref[h] = jnp.exp((idx + 1.0) * log_g)
            kdec_ref[h] = jnp.exp((chunk - 1.0 - idx) * log_g)

    @pl.when(c == 0)
    def _():
        state_ref[...] = jnp.zeros(state_ref.shape, F32)

    for h in range(N_RET_HEADS):
        sl = slice(h * RET_KEY_DIM, (h + 1) * RET_KEY_DIM)
        q = q_ref[:, sl]
        k = k_ref[:, sl]
        v = v_ref[:, sl]
        scores = lax.dot_general(q, k, (((1,), (1,)), ((), ())), preferred_element_type=F32)
        inner = jnp.dot((scores * mask_ref[h]).astype(BF16), v, preferred_element_type=F32)
        state = state_ref[h]
        cross = jnp.dot(q, state.astype(BF16), preferred_element_type=F32) * qdec_ref[h]
        k_dec = (k.astype(F32) * kdec_ref[h]).astype(BF16)
        state_ref[h] = (state * math.exp(chunk * _ret_log_decay(h))
                        + lax.dot_general(k_dec, v, (((0,), (0,)), ((), ())), preferred_element_type=F32))
        o = inner + cross
        o = o * lax.rsqrt(jnp.mean(o * o, axis=-1, keepdims=True) + EPS)
        o_ref[:, sl] = (o * _silu(g_ref[:, sl])).astype(BF16)


def _retention(proj, gate, *, batch, seq, chunk):
    t = proj.shape[0]
    assert seq % chunk == 0
    nc = seq // chunk
    width = N_RET_HEADS * RET_V_DIM
    return pl.pallas_call(
        functools.partial(_retention_kernel, chunk=chunk),
        grid=(batch, nc),
        in_specs=[
            pl.BlockSpec((chunk, PROJ_BLOCK), lambda b, c: (b * nc + c, BLK_RQ)),
            pl.BlockSpec((chunk, PROJ_BLOCK), lambda b, c: (b * nc + c, BLK_RK)),
            pl.BlockSpec((chunk, PROJ_BLOCK), lambda b, c: (b * nc + c, BLK_RV)),
            pl.BlockSpec((chunk, PROJ_BLOCK), lambda b, c: (b * nc + c, 0)),
        ],
        out_specs=pl.BlockSpec((chunk, width), lambda b, c: (b * nc + c, 0)),
        out_shape=jax.ShapeDtypeStruct((t, width), BF16),
        scratch_shapes=[
            pltpu.VMEM((N_RET_HEADS, RET_KEY_DIM, RET_V_DIM), F32),
            pltpu.VMEM((N_RET_HEADS, chunk, chunk), F32),
            pltpu.VMEM((N_RET_HEADS, chunk, RET_KEY_DIM), F32),
            pltpu.VMEM((N_RET_HEADS, chunk, RET_KEY_DIM), F32),
        ],
        compiler_params=_params("arbitrary", "arbitrary"),
        name="retention",
    )(proj, proj, proj, gate)


def _out_proj_kernel(x_ref, a_ref, r_ref, w_ref, o_ref):
    ka = a_ref.shape[1]
    y = jnp.dot(a_ref[...], w_ref[:ka, :], preferred_element_type=F32)
    y = y + jnp.dot(r_ref[...], w_ref[ka:, :], preferred_element_type=F32)
    o_ref[...] = x_ref[...] + y


def _out_proj(x, a, r, w, layer, *, tm, tn):
    t, d = x.shape
    ka, kr = a.shape[1], r.shape[1]
    assert t % tm == 0 and d % tn == 0 and w.shape[1:] == (ka + kr, d)
    w_mode = pl.Buffered(1) if tn == d else None
    return pl.pallas_call(
        _out_proj_kernel,
        grid=(t // tm, d // tn),
        in_specs=[
            pl.BlockSpec((tm, tn), lambda i, j: (i, j)),
            pl.BlockSpec((tm, ka), lambda i, j: (i, 0)),
            pl.BlockSpec((tm, kr), lambda i, j: (i, 0)),
            pl.BlockSpec((None, ka + kr, tn), lambda i, j: (layer, 0, j), pipeline_mode=w_mode),
        ],
        out_specs=pl.BlockSpec((tm, tn), lambda i, j: (i, j)),
        out_shape=jax.ShapeDtypeStruct((t, d), F32),
        compiler_params=_params("arbitrary", "arbitrary"),
        name="out_proj",
    )(x, a, r, w)


def _pool_kernel(x_ref, xp_ref, g_ref, w_ref, scale_ref, o_ref, h_ref, *, tm, seq):
    i = pl.program_id(0)
    start = (i * tm) % seq
    g = g_ref[...]
    h_prev = _rms_norm(xp_ref[...], g)
    h_ref[0:POOL_HALO, :] = jnp.where(start == 0, 0.0, h_prev)
    h_ref[POOL_HALO:, :] = _rms_norm(x_ref[...], g)
    pos = start + lax.broadcasted_iota(jnp.int32, (tm, 1), 0)
    gd = w_ref.shape[1]
    for gi, win in enumerate(POOL_WINDOWS):
        cols = slice(gi * gd, (gi + 1) * gd)
        total = h_ref[:, cols]
        hg = total[POOL_HALO:]
        span = 1
        while span < win:
            total = total + pltpu.roll(total, span, 0)
            span *= 2
        inv_count = 1.0 / jnp.minimum(pos + 1, win).astype(F32)
        pooled = total[POOL_HALO:] * inv_count - hg
        y = jnp.dot(pooled.astype(BF16), w_ref[gi], preferred_element_type=F32)
        o_ref[:, cols] = x_ref[:, cols] + y * scale_ref[:, cols]


def _pool_mixer(x, g, w, layer, scale, *, seq, tm):
    t, d = x.shape
    assert t % tm == 0 and seq % tm == 0 and tm % POOL_HALO == 0 and max(POOL_WINDOWS) <= POOL_HALO
    assert all(win & (win - 1) == 0 for win in POOL_WINDOWS)
    halo_blocks = tm // POOL_HALO
    return pl.pallas_call(
        functools.partial(_pool_kernel, tm=tm, seq=seq),
        grid=(t // tm,),
        in_specs=[
            pl.BlockSpec((tm, d), lambda i: (i, 0)),
            pl.BlockSpec((POOL_HALO, d), lambda i: (jnp.maximum(i * halo_blocks - 1, 0), 0)),
            pl.BlockSpec((1, d), lambda i: (0, 0)),
            pl.BlockSpec((None,) + w.shape[1:], lambda i: (layer, 0, 0, 0)),
            pl.BlockSpec((1, d), lambda i: (0, 0)),
        ],
        out_specs=pl.BlockSpec((tm, d), lambda i: (i, 0)),
        out_shape=jax.ShapeDtypeStruct((t, d), F32),
        scratch_shapes=[pltpu.VMEM((tm + POOL_HALO, d), F32)],
        compiler_params=_params("arbitrary"),
        name="pool_mixer",
    )(x, x, g, w, scale)


def _conv(u, prev, cw_ref, cb_ref):
    ext = jnp.concatenate([prev, u], axis=0)
    y = cb_ref[...] + u * cw_ref[CONV_WIDTH - 1:CONV_WIDTH, :]
    for tap in range(1, CONV_WIDTH):
        shifted = pltpu.roll(ext, tap, 0)[CONV_HALO:]
        y = y + shifted * cw_ref[CONV_WIDTH - 1 - tap:CONV_WIDTH - tap, :]
    return y


def _ffn_kernel(*refs, tm, seq, final):
    if final:
        (x_ref, g_ref, wv_ref, wg_ref, cwv_ref, cwg_ref, cbv_ref, cbg_ref, wd_ref, gf_ref,
         o_ref, h_ref, tail_v_ref, tail_g_ref) = refs
    else:
        (x_ref, g_ref, wv_ref, wg_ref, cwv_ref, cwg_ref, cbv_ref, cbg_ref, wd_ref, wu_next_ref, wd_next_ref,
         o_ref, wu_cast_ref, wd_cast_ref, h_ref, tail_v_ref, tail_g_ref) = refs
    i = pl.program_id(0)
    f = pl.program_id(1)
    n_f = pl.num_programs(1)
    tsub = tm // FFN_SUB_TILES

    @pl.when(f == 0)
    def _():
        x = x_ref[...]
        h_ref[...] = _rms_norm(x, g_ref[...]).astype(BF16)
        o_ref[...] = x

    @pl.when((i * tm) % seq == 0)
    def _():
        tail_v_ref[f] = jnp.zeros(tail_v_ref.shape[1:], F32)
        tail_g_ref[f] = jnp.zeros(tail_g_ref.shape[1:], F32)

    def up(s):
        h_s = h_ref[s * tsub:(s + 1) * tsub, :]
        return (jnp.dot(h_s, wv_ref[...], preferred_element_type=F32),
                jnp.dot(h_s, wg_ref[...], preferred_element_type=F32))

    prev = (tail_v_ref[f], tail_g_ref[f])
    u_cur = up(0)
    for s in range(FFN_SUB_TILES):
        u_next = up(s + 1) if s + 1 < FFN_SUB_TILES else None
        val = _conv(u_cur[0], prev[0], cwv_ref, cbv_ref)
        gate = _conv(u_cur[1], prev[1], cwg_ref, cbg_ref)
        act = (_silu(gate) * val).astype(BF16)
        o_ref[s * tsub:(s + 1) * tsub, :] += jnp.dot(act, wd_ref[...], preferred_element_type=F32)
        prev = (u_cur[0][tsub - CONV_HALO:], u_cur[1][tsub - CONV_HALO:])
        u_cur = u_next
    tail_v_ref[f] = prev[0]
    tail_g_ref[f] = prev[1]

    if final:
        @pl.when(f == n_f - 1)
        def _():
            o_ref[...] = _rms_norm(o_ref[...], gf_ref[...])
    else:
        wu_cast_ref[...] = wu_next_ref[...].astype(BF16)
        wd_cast_ref[...] = wd_next_ref[...].astype(BF16)


def _conv_ffn(x, g, w_up, conv_w, conv_b, w_down, layer, final_g, next_weights, *, seq, tm, tf):
    t, d = x.shape
    d_ff = w_down.shape[1]
    assert t % tm == 0 and seq % tm == 0 and d_ff % tf == 0 and w_up.shape[1:] == (d, 2 * d_ff)
    assert tm % FFN_SUB_TILES == 0 and (tm // FFN_SUB_TILES) % 16 == 0
    n_f = d_ff // tf
    steps = (t // tm) * n_f
    final = final_g is not None
    assert final != (next_weights is not None)
    in_specs = [
        pl.BlockSpec((tm, d), lambda i, f: (i, 0)),
        pl.BlockSpec((1, d), lambda i, f: (0, 0)),
        pl.BlockSpec((None, d, tf), lambda i, f: (layer, 0, f)),
        pl.BlockSpec((None, d, tf), lambda i, f: (layer, 0, n_f + f)),
        pl.BlockSpec((None, CONV_WIDTH, tf), lambda i, f: (layer, 0, f)),
        pl.BlockSpec((None, CONV_WIDTH, tf), lambda i, f: (layer, 0, n_f + f)),
        pl.BlockSpec((None, 1, tf), lambda i, f: (layer, 0, f)),
        pl.BlockSpec((None, 1, tf), lambda i, f: (layer, 0, n_f + f)),
        pl.BlockSpec((None, tf, d), lambda i, f: (layer, f, 0)),
    ]
    args = [x, g, w_up, w_up, conv_w, conv_w, conv_b, conv_b, w_down]
    out_specs = [pl.BlockSpec((tm, d), lambda i, f: (i, 0))]
    out_shape = [jax.ShapeDtypeStruct((t, d), F32)]
    if final:
        in_specs.append(pl.BlockSpec((1, d), lambda i, f: (0, 0)))
        args.append(final_g)
    else:
        w_up_next, w_down_next, next_layer = next_weights
        up_rows = d * 2 * d_ff // CAST_WIDTH
        assert up_rows % (16 * steps) == 0 and d_ff % (16 * steps) == 0
        ru, rd = up_rows // steps, d_ff // steps
        in_specs += [
            pl.BlockSpec((None, ru, CAST_WIDTH), lambda i, f: (next_layer, i * n_f + f, 0)),
            pl.BlockSpec((None, rd, d), lambda i, f: (next_layer, i * n_f + f, 0)),
        ]
        args += [w_up_next.reshape(w_up_next.shape[0], up_rows, CAST_WIDTH), w_down_next]
        out_specs += [
            pl.BlockSpec((ru, CAST_WIDTH), lambda i, f: (i * n_f + f, 0)),
            pl.BlockSpec((rd, d), lambda i, f: (i * n_f + f, 0)),
        ]
        out_shape += [jax.ShapeDtypeStruct((up_rows, CAST_WIDTH), BF16), jax.ShapeDtypeStruct((d_ff, d), BF16)]
    outs = pl.pallas_call(
        functools.partial(_ffn_kernel, tm=tm, seq=seq, final=final),
        grid=(t // tm, n_f),
        in_specs=in_specs,
        out_specs=out_specs,
        out_shape=out_shape,
        scratch_shapes=[
            pltpu.VMEM((tm, d), BF16),
            pltpu.VMEM((n_f, CONV_HALO, tf), F32),
            pltpu.VMEM((n_f, CONV_HALO, tf), F32),
        ],
        compiler_params=_params("arbitrary", "arbitrary"),
        name="conv_ffn",
    )(*args)
    if final:
        return outs[0], None
    return outs[0], (outs[1].reshape(1, d, 2 * d_ff), outs[2].reshape(1, d_ff, d))


def _tiles(seq):
    pick = lambda want: min(want, seq)
    return dict(proj_tm=pick(1024), attn_ta=pick(512), ret_chunk=pick(512), out_tm=pick(512),
                pool_tm=pick(1024), ffn_tm=pick(1024))


def kernel(x, rel_bias, mix_norm, ffn_norm, final_norm, w_in, w_out, lam_q1, lam_k1, lam_q2, lam_k2,
           diff_subln, pool_w, pool_scale, w_up, conv_w, conv_b, w_down):
    batch, seq, d = x.shape
    depth = mix_norm.shape[0]
    tiles = _tiles(seq)
    xf = x.reshape(batch * seq, d)

    pos = jnp.arange(seq, dtype=F32)
    inv_freq = 1.0 / (ROPE_BASE ** jnp.linspace(0.0, 1.0, RET_KEY_DIM // 2, dtype=F32))
    ang = pos[:, None] * inv_freq[None, :]
    cos, sin = jnp.cos(ang), jnp.sin(ang)

    w_in_b, w_out_b, pool_w_b = w_in.astype(BF16), w_out.astype(BF16), pool_w.astype(BF16)
    ffn_w = (w_up[:1].astype(BF16), w_down[:1].astype(BF16))
    conv_b3 = conv_b[:, None, :]

    for l in range(depth):
        g_mix = mix_norm[l][None, :]
        if l % 2 == 0:
            i = l // 2
            lam_init = 0.8 - 0.6 * math.exp(-0.3 * l)
            proj, gate = _in_proj(xf, g_mix, w_in_b, i, cos, sin, seq=seq, tm=tiles["proj_tm"])
            lam_params = jnp.stack([lam_q1[i], lam_k1[i], lam_q2[i], lam_k2[i]]).astype(F32)
            a_out = _diff_attention(proj, rel_bias, lam_params, diff_subln[i][None, :], batch=batch, seq=seq,
                                    ta=tiles["attn_ta"], lam_init=lam_init)
            r_out = _retention(proj, gate, batch=batch, seq=seq, chunk=tiles["ret_chunk"])
            xf = _out_proj(xf, a_out, r_out, w_out_b, i, tm=tiles["out_tm"], tn=d)
        else:
            j = l // 2
            xf = _pool_mixer(xf, g_mix, pool_w_b, j, pool_scale[j][None, :], seq=seq, tm=tiles["pool_tm"])
        last = l == depth - 1
        xf, ffn_w = _conv_ffn(xf, ffn_norm[l][None, :], ffn_w[0], conv_w[l:l + 1], conv_b3[l:l + 1], ffn_w[1], 0,
                              final_norm[None, :] if last else None,
                              None if last else (w_up, w_down, l + 1), seq=seq, tm=tiles["ffn_tm"], tf=512)
    return xf.reshape(batch, seq, d)
```

```python
import functools
import math

import jax
import jax.numpy as jnp
from jax import lax
from jax.experimental import pallas as pl
from jax.experimental.pallas import tpu as pltpu

F32 = jnp.float32
BF16 = jnp.bfloat16

N_DIFF_HEADS = 4
DIFF_HEAD_DIM = 128
DIFF_V_DIM = 2 * DIFF_HEAD_DIM
N_DIFF_MAPS = 2 * N_DIFF_HEADS
N_RET_HEADS = 4
RET_KEY_DIM = 256
RET_V_DIM = 256
ROPE_BASE = 10000.0
NUM_BUCKETS = 32
MAX_DISTANCE = 128
POOL_WINDOWS = (2, 4, 8, 16)
POOL_HALO = 16
CONV_WIDTH = 3
CONV_HALO = 8
EPS = 1e-6
NEG_INF = -1e30
LOG2_E = math.log2(math.e)
PROJ_BLOCK = 1024
FFN_SUB_TILES = 2
(BLK_AQ, BLK_AK, BLK_AV, BLK_RQ, BLK_RK, BLK_RV, BLK_RG) = range(7)

LANES = 128
VMEM_LIMIT = 56 * 1024 * 1024


def _params(*semantics):
    return pltpu.CompilerParams(dimension_semantics=semantics, vmem_limit_bytes=VMEM_LIMIT)


def _rms_norm(xf, g):
    y = xf * lax.rsqrt(jnp.mean(xf * xf, axis=-1, keepdims=True) + EPS)
    return y * g


def _silu(z):
    half = 0.5 * z
    return half + half * jnp.tanh(half)


def _in_proj_kernel(x_ref, g_ref, w_ref, cos_ref, sin_ref, proj_ref, gate_ref, h_ref):
    j = pl.program_id(1)

    @pl.when(j == 0)
    def _():
        h_ref[...] = _rms_norm(x_ref[...], g_ref[...]).astype(BF16)

    acc = jnp.dot(h_ref[...], w_ref[...], preferred_element_type=F32)

    def rotary(scale):
        cos = cos_ref[...]
        sin = sin_ref[...]
        half = RET_KEY_DIM // 2
        for lo in range(0, PROJ_BLOCK, RET_KEY_DIM):
            t1 = acc[:, lo:lo + half]
            t2 = acc[:, lo + half:lo + RET_KEY_DIM]
            proj_ref[:, lo:lo + half] = ((t1 * cos - t2 * sin) * scale).astype(BF16)
            proj_ref[:, lo + half:lo + RET_KEY_DIM] = ((t1 * sin + t2 * cos) * scale).astype(BF16)

    @pl.when(j == BLK_AQ)
    def _():
        proj_ref[...] = (acc * (DIFF_HEAD_DIM ** -0.5 * LOG2_E)).astype(BF16)

    @pl.when((j == BLK_AK) | (j == BLK_AV) | (j == BLK_RV))
    def _():
        proj_ref[...] = acc.astype(BF16)

    @pl.when(j == BLK_RQ)
    def _():
        rotary(1.0)

    @pl.when(j == BLK_RK)
    def _():
        rotary(RET_KEY_DIM ** -0.5)

    @pl.when(j == BLK_RG)
    def _():
        gate_ref[...] = acc


def _in_proj(x, g, w, layer, cos, sin, *, seq, tm):
    t, d = x.shape
    n_blk = w.shape[2] // PROJ_BLOCK
    assert n_blk == 7 and t % tm == 0 and seq % tm == 0
    tiles_per_seq = seq // tm
    return pl.pallas_call(
        _in_proj_kernel,
        grid=(t // tm, n_blk),
        in_specs=[
            pl.BlockSpec((tm, d), lambda i, j: (i, 0)),
            pl.BlockSpec((1, d), lambda i, j: (0, 0)),
            pl.BlockSpec((None, d, PROJ_BLOCK), lambda i, j: (layer, 0, j)),
            pl.BlockSpec((tm, RET_KEY_DIM // 2), lambda i, j: (i % tiles_per_seq, 0)),
            pl.BlockSpec((tm, RET_KEY_DIM // 2), lambda i, j: (i % tiles_per_seq, 0)),
        ],
        out_specs=[
            pl.BlockSpec((tm, PROJ_BLOCK), lambda i, j: (i, jnp.minimum(j, BLK_RV))),
            pl.BlockSpec((tm, PROJ_BLOCK), lambda i, j: (i, 0)),
        ],
        out_shape=[
            jax.ShapeDtypeStruct((t, BLK_RG * PROJ_BLOCK), BF16),
            jax.ShapeDtypeStruct((t, PROJ_BLOCK), F32),
        ],
        scratch_shapes=[pltpu.VMEM((tm, d), BF16)],
        compiler_params=_params("arbitrary", "arbitrary"),
        name="in_proj",
    )(x, g, w, cos, sin)


def _t5_bucket(dist):
    max_exact = NUM_BUCKETS // 2
    is_small = dist < max_exact
    df = jnp.maximum(dist, 1).astype(F32)
    large = max_exact + (jnp.log(df / max_exact) / math.log(MAX_DISTANCE / max_exact)
                         * (NUM_BUCKETS - max_exact)).astype(jnp.int32)
    large = jnp.minimum(large, NUM_BUCKETS - 1)
    return jnp.where(is_small, dist, large)


def _attn_kernel(qi_ref, ki_ref, q_ref, k_ref, v_ref, bias_ref, lam_ref, subln_ref, o_ref,
                 acc_ref, m_ref, l_ref, band_ref, *, ta, lam_init):
    b = pl.program_id(0)
    p = pl.program_id(1)
    qi = qi_ref[p]
    ki = ki_ref[p]
    n_sub = ta // LANES

    @pl.when((b == 0) & (p == 0))
    def _():
        row = lax.broadcasted_iota(jnp.int32, (LANES, LANES), 0)
        col = lax.broadcasted_iota(jnp.int32, (LANES, LANES), 1)
        for slot in range(2):
            bucket = _t5_bucket(jnp.maximum(row - col + slot * LANES, 0))
            for m in range(N_DIFF_MAPS):
                val = jnp.zeros((LANES, LANES), F32)
                for bk in range(NUM_BUCKETS):
                    val = jnp.where(bucket == bk, bias_ref[bk, m], val)
                band_ref[slot, m] = (val - bias_ref[NUM_BUCKETS - 1, m]) * LOG2_E

    @pl.when(ki == 0)
    def _():
        m_ref[...] = jnp.full(m_ref.shape, NEG_INF, F32)
        l_ref[...] = jnp.zeros(l_ref.shape, F32)
        acc_ref[...] = jnp.zeros(acc_ref.shape, F32)

    def biased_slabs(s, m, kind):
        slabs = []
        for c in range(n_sub):
            sc = s[:, c * LANES:(c + 1) * LANES]
            if kind == "far" or (kind == "prev" and c != n_sub - 1):
                slabs.append(sc)
                continue
            pieces = []
            for a in range(n_sub):
                blk = sc[a * LANES:(a + 1) * LANES]
                if kind == "prev":
                    if a == 0:
                        blk = blk + band_ref[1, m]
                elif a < c:
                    blk = jnp.full((LANES, LANES), NEG_INF, F32)
                elif a == c:
                    row = lax.broadcasted_iota(jnp.int32, (LANES, LANES), 0)
                    col = lax.broadcasted_iota(jnp.int32, (LANES, LANES), 1)
                    blk = jnp.where(row >= col, blk + band_ref[0, m], NEG_INF)
                elif a == c + 1:
                    blk = blk + band_ref[1, m]
                pieces.append(blk)
            slabs.append(jnp.concatenate(pieces, axis=0))
        return slabs

    def scores(m):
        q_m = q_ref[:, m * DIFF_HEAD_DIM:(m + 1) * DIFF_HEAD_DIM]
        k_m = k_ref[:, m * DIFF_HEAD_DIM:(m + 1) * DIFF_HEAD_DIM]
        return lax.dot_general(q_m, k_m, (((1,), (1,)), ((), ())), preferred_element_type=F32)

    def step(kind):
        for m in range(N_DIFF_MAPS):
            s = scores(m)
            h = m // 2
            v_h = v_ref[:, h * DIFF_V_DIM:(h + 1) * DIFF_V_DIM]
            slabs = biased_slabs(s, m, kind)
            part_max = slabs[0]
            for sc in slabs[1:]:
                part_max = jnp.maximum(part_max, sc)
            m_prev = m_ref[m]
            m_new = jnp.maximum(m_prev, jnp.max(part_max, axis=1, keepdims=True))
            alpha = jnp.exp2(m_prev - m_new)
            probs = [jnp.exp2(sc - m_new) for sc in slabs]
            part_sum = probs[0]
            for pc in probs[1:]:
                part_sum = part_sum + pc
            m_ref[m] = m_new
            l_ref[m] = alpha * l_ref[m] + part_sum
            pv = jnp.dot(jnp.concatenate([pc.astype(BF16) for pc in probs], axis=1), v_h,
                         preferred_element_type=F32)
            alpha_v = jnp.concatenate([alpha] * (DIFF_V_DIM // LANES), axis=1)
            acc_ref[m] = acc_ref[m] * alpha_v + pv

    @pl.when(ki < qi - 1)
    def _():
        step("far")

    @pl.when((ki == qi - 1))
    def _():
        step("prev")

    @pl.when(ki == qi)
    def _():
        step("diag")
        lam_p = lam_ref[...]
        lam = (jnp.exp(jnp.sum(lam_p[0:1] * lam_p[1:2], axis=1, keepdims=True))
               - jnp.exp(jnp.sum(lam_p[2:3] * lam_p[3:4], axis=1, keepdims=True)) + lam_init)
        for h in range(N_DIFF_HEADS):
            inv0 = 1.0 / jnp.sum(l_ref[2 * h], axis=1, keepdims=True)
            inv1 = 1.0 / jnp.sum(l_ref[2 * h + 1], axis=1, keepdims=True)
            o = acc_ref[2 * h] * inv0 - lam * (acc_ref[2 * h + 1] * inv1)
            o = _rms_norm(o, subln_ref[...]) * (1.0 - lam_init)
            o_ref[:, h * DIFF_V_DIM:(h + 1) * DIFF_V_DIM] = o.astype(BF16)


def _diff_attention(proj, rel_bias, lam_params, subln, *, batch, seq, ta, lam_init):
    t = proj.shape[0]
    assert seq % ta == 0 and ta % LANES == 0 and ta >= 2 * LANES
    nq = seq // ta
    pairs = [(q, k) for q in range(nq) for k in range(q + 1)]
    qi = jnp.asarray([q for q, _ in pairs], jnp.int32)
    ki = jnp.asarray([k for _, k in pairs], jnp.int32)
    width = N_DIFF_HEADS * DIFF_V_DIM
    grid_spec = pltpu.PrefetchScalarGridSpec(
        num_scalar_prefetch=2,
        grid=(batch, len(pairs)),
        in_specs=[
            pl.BlockSpec((ta, PROJ_BLOCK), lambda b, p, qi, ki: (b * nq + qi[p], BLK_AQ)),
            pl.BlockSpec((ta, PROJ_BLOCK), lambda b, p, qi, ki: (b * nq + ki[p], BLK_AK)),
            pl.BlockSpec((ta, PROJ_BLOCK), lambda b, p, qi, ki: (b * nq + ki[p], BLK_AV)),
            pl.BlockSpec(memory_space=pltpu.SMEM),
            pl.BlockSpec((4, DIFF_HEAD_DIM), lambda b, p, qi, ki: (0, 0)),
            pl.BlockSpec((1, DIFF_V_DIM), lambda b, p, qi, ki: (0, 0)),
        ],
        out_specs=pl.BlockSpec((ta, width), lambda b, p, qi, ki: (b * nq + qi[p], 0)),
        scratch_shapes=[
            pltpu.VMEM((N_DIFF_MAPS, ta, DIFF_V_DIM), F32),
            pltpu.VMEM((N_DIFF_MAPS, ta, LANES), F32),
            pltpu.VMEM((N_DIFF_MAPS, ta, LANES), F32),
            pltpu.VMEM((2, N_DIFF_MAPS, LANES, LANES), F32),
        ],
    )
    return pl.pallas_call(
        functools.partial(_attn_kernel, ta=ta, lam_init=lam_init),
        grid_spec=grid_spec,
        out_shape=jax.ShapeDtypeStruct((t, width), BF16),
        compiler_params=_params("arbitrary", "arbitrary"),
        name="diff_attention",
    )(qi, ki, proj, proj, proj, rel_bias, lam_params, subln)


def _ret_log_decay(h):
    return math.log(1.0 - 2.0 ** (-5.0 - h))


def _retention_kernel(q_ref, k_ref, v_ref, g_ref, o_ref, state_ref, mask_ref, qdec_ref, kdec_ref, *, chunk):
    b = pl.program_id(0)
    c = pl.program_id(1)

    @pl.when((b == 0) & (c == 0))
    def _():
        rel = (lax.broadcasted_iota(jnp.int32, (chunk, chunk), 0)
               - lax.broadcasted_iota(jnp.int32, (chunk, chunk), 1)).astype(F32)
        idx = lax.broadcasted_iota(jnp.int32, (chunk, RET_KEY_DIM), 0).astype(F32)
        for h in range(N_RET_HEADS):
            log_g = _ret_log_decay(h)
            mask_ref[h] = jnp.where(rel >= 0, jnp.exp(jnp.maximum(rel, 0.0) * log_g), 0.0)
            qdec_ref[h] = jnp.exp((idx + 1.0) * log_g)
            kdec_ref[h] = jnp.exp((chunk - 1.0 - idx) * log_g)

    @pl.when(c == 0)
    def _():
        state_ref[...] = jnp.zeros(state_ref.shape, F32)

    for h in range(N_RET_HEADS):
        sl = slice(h * RET_KEY_DIM, (h + 1) * RET_KEY_DIM)
        q = q_ref[:, sl]
        k = k_ref[:, sl]
        v = v_ref[:, sl]
        scores = lax.dot_general(q, k, (((1,), (1,)), ((), ())), preferred_element_type=F32)
        inner = jnp.dot((scores * mask_ref[h]).astype(BF16), v, preferred_element_type=F32)
        state = state_ref[h]
        cross = jnp.dot(q, state.astype(BF16), preferred_element_type=F32) * qdec_ref[h]
        k_dec = (k.astype(F32) * kdec_ref[h]).astype(BF16)
        state_ref[h] = (state * math.exp(chunk * _ret_log_decay(h))
                        + lax.dot_general(k_dec, v, (((0,), (0,)), ((), ())), preferred_element_type=F32))
        o = inner + cross
        o = o * lax.rsqrt(jnp.mean(o * o, axis=-1, keepdims=True) + EPS)
        o_ref[:, sl] = (o * _silu(g_ref[:, sl])).astype(BF16)


def _retention(proj, gate, *, batch, seq, chunk):
    t = proj.shape[0]
    assert seq % chunk == 0
    nc = seq // chunk
    width = N_RET_HEADS * RET_V_DIM
    return pl.pallas_call(
        functools.partial(_retention_kernel, chunk=chunk),
        grid=(batch, nc),
        in_specs=[
            pl.BlockSpec((chunk, PROJ_BLOCK), lambda b, c: (b * nc + c, BLK_RQ)),
            pl.BlockSpec((chunk, PROJ_BLOCK), lambda b, c: (b * nc + c, BLK_RK)),
            pl.BlockSpec((chunk, PROJ_BLOCK), lambda b, c: (b * nc + c, BLK_RV)),
            pl.BlockSpec((chunk, PROJ_BLOCK), lambda b, c: (b * nc + c, 0)),
        ],
        out_specs=pl.BlockSpec((chunk, width), lambda b, c: (b * nc + c, 0)),
        out_shape=jax.ShapeDtypeStruct((t, width), BF16),
        scratch_shapes=[
            pltpu.VMEM((N_RET_HEADS, RET_KEY_DIM, RET_V_DIM), F32),
            pltpu.VMEM((N_RET_HEADS, chunk, chunk), F32),
            pltpu.VMEM((N_RET_HEADS, chunk, RET_KEY_DIM), F32),
            pltpu.VMEM((N_RET_HEADS, chunk, RET_KEY_DIM), F32),
        ],
        compiler_params=_params("arbitrary", "arbitrary"),
        name="retention",
    )(proj, proj, proj, gate)


def _out_proj_kernel(x_ref, a_ref, r_ref, w_ref, o_ref):
    ka = a_ref.shape[1]
    y = jnp.dot(a_ref[...], w_ref[:ka, :], preferred_element_type=F32)
    y = y + jnp.dot(r_ref[...], w_ref[ka:, :], preferred_element_type=F32)
    o_ref[...] = x_ref[...] + y


def _out_proj(x, a, r, w, layer, *, tm, tn):
    t, d = x.shape
    ka, kr = a.shape[1], r.shape[1]
    assert t % tm == 0 and d % tn == 0 and w.shape[1:] == (ka + kr, d)
    w_mode = pl.Buffered(1) if tn == d else None
    return pl.pallas_call(
        _out_proj_kernel,
        grid=(t // tm, d // tn),
        in_specs=[
            pl.BlockSpec((tm, tn), lambda i, j: (i, j)),
            pl.BlockSpec((tm, ka), lambda i, j: (i, 0)),
            pl.BlockSpec((tm, kr), lambda i, j: (i, 0)),
            pl.BlockSpec((None, ka + kr, tn), lambda i, j: (layer, 0, j), pipeline_mode=w_mode),
        ],
        out_specs=pl.BlockSpec((tm, tn), lambda i, j: (i, j)),
        out_shape=jax.ShapeDtypeStruct((t, d), F32),
        compiler_params=_params("arbitrary", "arbitrary"),
        name="out_proj",
    )(x, a, r, w)


def _pool_kernel(x_ref, xp_ref, g_ref, w_ref, scale_ref, o_ref, h_ref, *, tm, seq):
    i = pl.program_id(0)
    start = (i * tm) % seq
    g = g_ref[...]
    h_prev = _rms_norm(xp_ref[...], g)
    h_ref[0:POOL_HALO, :] = jnp.where(start == 0, 0.0, h_prev)
    h_ref[POOL_HALO:, :] = _rms_norm(x_ref[...], g)
    pos = start + lax.broadcasted_iota(jnp.int32, (tm, 1), 0)
    gd = w_ref.shape[1]
    for gi, win in enumerate(POOL_WINDOWS):
        cols = slice(gi * gd, (gi + 1) * gd)
        total = h_ref[:, cols]
        hg = total[POOL_HALO:]
        span = 1
        while span < win:
            total = total + pltpu.roll(total, span, 0)
            span *= 2
        inv_count = 1.0 / jnp.minimum(pos + 1, win).astype(F32)
        pooled = total[POOL_HALO:] * inv_count - hg
        y = jnp.dot(pooled.astype(BF16), w_ref[gi], preferred_element_type=F32)
        o_ref[:, cols] = x_ref[:, cols] + y * scale_ref[:, cols]


def _pool_mixer(x, g, w, layer, scale, *, seq, tm):
    t, d = x.shape
    assert t % tm == 0 and seq % tm == 0 and tm % POOL_HALO == 0 and max(POOL_WINDOWS) <= POOL_HALO
    assert all(win & (win - 1) == 0 for win in POOL_WINDOWS)
    halo_blocks = tm // POOL_HALO
    return pl.pallas_call(
        functools.partial(_pool_kernel, tm=tm, seq=seq),
        grid=(t // tm,),
        in_specs=[
            pl.BlockSpec((tm, d), lambda i: (i, 0)),
            pl.BlockSpec((POOL_HALO, d), lambda i: (jnp.maximum(i * halo_blocks - 1, 0), 0)),
            pl.BlockSpec((1, d), lambda i: (0, 0)),
            pl.BlockSpec((None,) + w.shape[1:], lambda i: (layer, 0, 0, 0)),
            pl.BlockSpec((1, d), lambda i: (0, 0)),
        ],
        out_specs=pl.BlockSpec((tm, d), lambda i: (i, 0)),
        out_shape=jax.ShapeDtypeStruct((t, d), F32),
        scratch_shapes=[pltpu.VMEM((tm + POOL_HALO, d), F32)],
        compiler_params=_params("arbitrary"),
        name="pool_mixer",
    )(x, x, g, w, scale)


def _conv(u, prev, cw_ref, cb_ref):
    ext = jnp.concatenate([prev, u], axis=0)
    y = cb_ref[...] + u * cw_ref[CONV_WIDTH - 1:CONV_WIDTH, :]
    for tap in range(1, CONV_WIDTH):
        shifted = pltpu.roll(ext, tap, 0)[CONV_HALO:]
        y = y + shifted * cw_ref[CONV_WIDTH - 1 - tap:CONV_WIDTH - tap, :]
    return y


def _ffn_kernel(*refs, tm, seq, final):
    if final:
        (x_ref, g_ref, wv_ref, wg_ref, cwv_ref, cwg_ref, cbv_ref, cbg_ref, wd_ref, gf_ref,
         o_ref, h_ref, tail_v_ref, tail_g_ref) = refs
    else:
        (x_ref, g_ref, wv_ref, wg_ref, cwv_ref, cwg_ref, cbv_ref, cbg_ref, wd_ref, wu_next_ref, wd_next_ref,
         o_ref, wu_cast_ref, wd_cast_ref, h_ref, tail_v_ref, tail_g_ref) = refs
    i = pl.program_id(0)
    f = pl.program_id(1)
    n_f = pl.num_programs(1)
    tsub = tm // FFN_SUB_TILES

    @pl.when(f == 0)
    def _():
        x = x_ref[...]
        h_ref[...] = _rms_norm(x, g_ref[...]).astype(BF16)
        o_ref[...] = x

    @pl.when((i * tm) % seq == 0)
    def _():
        tail_v_ref[f] = jnp.zeros(tail_v_ref.shape[1:], F32)
        tail_g_ref[f] = jnp.zeros(tail_g_ref.shape[1:], F32)

    def up(s):
        h_s = h_ref[s * tsub:(s + 1) * tsub, :]
        return (jnp.dot(h_s, wv_ref[...], preferred_element_type=F32),
                jnp.dot(h_s, wg_ref[...], preferred_element_type=F32))

    prev = (tail_v_ref[f], tail_g_ref[f])
    u_cur = up(0)
    for s in range(FFN_SUB_TILES):
        u_next = up(s + 1) if s + 1 < FFN_SUB_TILES else None
        val = _conv(u_cur[0], prev[0], cwv_ref, cbv_ref)
        gate = _conv(u_cur[1], prev[1], cwg_ref, cbg_ref)
        act = (_silu(gate) * val).astype(BF16)
        o_ref[s * tsub:(s + 1) * tsub, :] += jnp.dot(act, wd_ref[...], preferred_element_type=F32)
        prev = (u_cur[0][tsub - CONV_HALO:], u_cur[1][tsub - CONV_HALO:])
        u_cur = u_next
    tail_v_ref[f] = prev[0]
    tail_g_ref[f] = prev[1]

    if final:
        @pl.when(f == n_f - 1)
        def _():
            o_ref[...] = _rms_norm(o_ref[...], gf_ref[...])
    else:
        wu_cast_ref[...] = wu_next_ref[...].astype(BF16)
        wd_cast_ref[...] = wd_next_ref[...].astype(BF16)


def _conv_ffn(x, g, w_up, conv_w, conv_b, w_down, layer, final_g, next_weights, *, seq, tm, tf):
    t, d = x.shape
    d_ff = w_down.shape[1]
    assert t % tm == 0 and seq % tm == 0 and d_ff % tf == 0 and w_up.shape[1:] == (d, 2 * d_ff)
    assert tm % FFN_SUB_TILES == 0 and (tm // FFN_SUB_TILES) % 16 == 0
    n_f = d_ff // tf
    steps = (t // tm) * n_f
    final = final_g is not None
    assert final != (next_weights is not None)
    in_specs = [
        pl.BlockSpec((tm, d), lambda i, f: (i, 0)),
        pl.BlockSpec((1, d), lambda i, f: (0, 0)),
        pl.BlockSpec((None, d, tf), lambda i, f: (layer, 0, f)),
        pl.BlockSpec((None, d, tf), lambda i, f: (layer, 0, n_f + f)),
        pl.BlockSpec((None, CONV_WIDTH, tf), lambda i, f: (layer, 0, f)),
        pl.BlockSpec((None, CONV_WIDTH, tf), lambda i, f: (layer, 0, n_f + f)),
        pl.BlockSpec((None, 1, tf), lambda i, f: (layer, 0, f)),
        pl.BlockSpec((None, 1, tf), lambda i, f: (layer, 0, n_f + f)),
        pl.BlockSpec((None, tf, d), lambda i, f: (layer, f, 0)),
    ]
    args = [x, g, w_up, w_up, conv_w, conv_w, conv_b, conv_b, w_down]
    out_specs = [pl.BlockSpec((tm, d), lambda i, f: (i, 0))]
    out_shape = [jax.ShapeDtypeStruct((t, d), F32)]
    if final:
        in_specs.append(pl.BlockSpec((1, d), lambda i, f: (0, 0)))
        args.append(final_g)
    else:
        w_up_next, w_down_next, next_layer = next_weights
        n_i = t // tm
        assert d % (16 * n_i) == 0 and (2 * d_ff) % (LANES * n_f) == 0 and d_ff % (16 * steps) == 0
        up_blk = (None, d // n_i, 2 * d_ff // n_f)
        down_blk = (None, d_ff // steps, d)
        in_specs += [
            pl.BlockSpec(up_blk, lambda i, f: (next_layer, i, f)),
            pl.BlockSpec(down_blk, lambda i, f: (next_layer, i * n_f + f, 0)),
        ]
        args += [w_up_next, w_down_next]
        out_specs += [
            pl.BlockSpec(up_blk, lambda i, f: (0, i, f)),
            pl.BlockSpec(down_blk, lambda i, f: (0, i * n_f + f, 0)),
        ]
        out_shape += [jax.ShapeDtypeStruct((1, d, 2 * d_ff), BF16), jax.ShapeDtypeStruct((1, d_ff, d), BF16)]
    outs = pl.pallas_call(
        functools.partial(_ffn_kernel, tm=tm, seq=seq, final=final),
        grid=(t // tm, n_f),
        in_specs=in_specs,
        out_specs=out_specs,
        out_shape=out_shape,
        scratch_shapes=[
            pltpu.VMEM((tm, d), BF16),
            pltpu.VMEM((n_f, CONV_HALO, tf), F32),
            pltpu.VMEM((n_f, CONV_HALO, tf), F32),
        ],
        compiler_params=_params("arbitrary", "arbitrary"),
        name="conv_ffn",
    )(*args)
    if final:
        return outs[0], None
    return outs[0], (outs[1], outs[2])


def _tiles(seq):
    pick = lambda want: min(want, seq)
    return dict(proj_tm=pick(1024), attn_ta=pick(512), ret_chunk=pick(512), out_tm=pick(512),
                pool_tm=pick(1024), ffn_tm=pick(1024))


def kernel(x, rel_bias, mix_norm, ffn_norm, final_norm, w_in, w_out, lam_q1, lam_k1, lam_q2, lam_k2,
           diff_subln, pool_w, pool_scale, w_up, conv_w, conv_b, w_down):
    batch, seq, d = x.shape
    depth = mix_norm.shape[0]
    tiles = _tiles(seq)
    xf = x.reshape(batch * seq, d)

    pos = jnp.arange(seq, dtype=F32)
    inv_freq = 1.0 / (ROPE_BASE ** jnp.linspace(0.0, 1.0, RET_KEY_DIM // 2, dtype=F32))
    ang = pos[:, None] * inv_freq[None, :]
    cos, sin = jnp.cos(ang), jnp.sin(ang)

    w_in_b, w_out_b, pool_w_b = w_in.astype(BF16), w_out.astype(BF16), pool_w.astype(BF16)
    ffn_w = (w_up[:1].astype(BF16), w_down[:1].astype(BF16))
    conv_b3 = conv_b[:, None, :]

    for l in range(depth):
        g_mix = mix_norm[l][None, :]
        if l % 2 == 0:
            i = l // 2
            lam_init = 0.8 - 0.6 * math.exp(-0.3 * l)
            proj, gate = _in_proj(xf, g_mix, w_in_b, i, cos, sin, seq=seq, tm=tiles["proj_tm"])
            lam_params = jnp.stack([lam_q1[i], lam_k1[i], lam_q2[i], lam_k2[i]]).astype(F32)
            a_out = _diff_attention(proj, rel_bias, lam_params, diff_subln[i][None, :], batch=batch, seq=seq,
                                    ta=tiles["attn_ta"], lam_init=lam_init)
            r_out = _retention(proj, gate, batch=batch, seq=seq, chunk=tiles["ret_chunk"])
            xf = _out_proj(xf, a_out, r_out, w_out_b, i, tm=tiles["out_tm"], tn=d)
        else:
            j = l // 2
            xf = _pool_mixer(xf, g_mix, pool_w_b, j, pool_scale[j][None, :], seq=seq, tm=tiles["pool_tm"])
        last = l == depth - 1
        xf, ffn_w = _conv_ffn(xf, ffn_norm[l][None, :], ffn_w[0], conv_w[l:l + 1], conv_b3[l:l + 1], ffn_w[1], 0,
                              final_norm[None, :] if last else None,
                              None if last else (w_up, w_down, l + 1), seq=seq, tm=tiles["ffn_tm"], tf=512)
    return xf.reshape(batch, seq, d)
```

```python
import functools
import math

import jax
import jax.numpy as jnp
from jax import lax
from jax.experimental import pallas as pl
from jax.experimental.pallas import tpu as pltpu

F32 = jnp.float32
BF16 = jnp.bfloat16

N_DIFF_HEADS = 4
DIFF_HEAD_DIM = 128
DIFF_V_DIM = 2 * DIFF_HEAD_DIM
N_DIFF_MAPS = 2 * N_DIFF_HEADS
N_RET_HEADS = 4
RET_KEY_DIM = 256
RET_V_DIM = 256
ROPE_BASE = 10000.0
NUM_BUCKETS = 32
MAX_DISTANCE = 128
POOL_WINDOWS = (2, 4, 8, 16)
POOL_HALO = 16
CONV_WIDTH = 3
CONV_HALO = 8
EPS = 1e-6
NEG_INF = -1e30
LOG2_E = math.log2(math.e)
PROJ_BLOCK = 1024
FFN_SUB_TILES = 2
(BLK_AQ, BLK_AK, BLK_AV, BLK_RQ, BLK_RK, BLK_RV, BLK_RG) = range(7)

LANES = 128
VMEM_LIMIT = 56 * 1024 * 1024


def _params(*semantics):
    return pltpu.CompilerParams(dimension_semantics=semantics, vmem_limit_bytes=VMEM_LIMIT)


def _rms_norm(xf, g):
    y = xf * lax.rsqrt(jnp.mean(xf * xf, axis=-1, keepdims=True) + EPS)
    return y * g


def _silu(z):
    half = 0.5 * z
    return half + half * jnp.tanh(half)


def _in_proj_kernel(x_ref, g_ref, w_ref, cos_ref, sin_ref, proj_ref, gate_ref, h_ref):
    j = pl.program_id(1)

    @pl.when(j == 0)
    def _():
        h_ref[...] = _rms_norm(x_ref[...], g_ref[...]).astype(BF16)

    acc = jnp.dot(h_ref[...], w_ref[...], preferred_element_type=F32)

    def rotary(scale):
        cos = cos_ref[...]
        sin = sin_ref[...]
        half = RET_KEY_DIM // 2
        for lo in range(0, PROJ_BLOCK, RET_KEY_DIM):
            t1 = acc[:, lo:lo + half]
            t2 = acc[:, lo + half:lo + RET_KEY_DIM]
            proj_ref[:, lo:lo + half] = ((t1 * cos - t2 * sin) * scale).astype(BF16)
            proj_ref[:, lo + half:lo + RET_KEY_DIM] = ((t1 * sin + t2 * cos) * scale).astype(BF16)

    @pl.when(j == BLK_AQ)
    def _():
        proj_ref[...] = (acc * (DIFF_HEAD_DIM ** -0.5 * LOG2_E)).astype(BF16)

    @pl.when((j == BLK_AK) | (j == BLK_AV) | (j == BLK_RV))
    def _():
        proj_ref[...] = acc.astype(BF16)

    @pl.when(j == BLK_RQ)
    def _():
        rotary(1.0)

    @pl.when(j == BLK_RK)
    def _():
        rotary(RET_KEY_DIM ** -0.5)

    @pl.when(j == BLK_RG)
    def _():
        gate_ref[...] = acc


def _in_proj(x, g, w, layer, cos, sin, *, seq, tm):
    t, d = x.shape
    n_blk = w.shape[2] // PROJ_BLOCK
    assert n_blk == 7 and t % tm == 0 and seq % tm == 0
    tiles_per_seq = seq // tm
    return pl.pallas_call(
        _in_proj_kernel,
        grid=(t // tm, n_blk),
        in_specs=[
            pl.BlockSpec((tm, d), lambda i, j: (i, 0)),
            pl.BlockSpec((1, d), lambda i, j: (0, 0)),
            pl.BlockSpec((None, d, PROJ_BLOCK), lambda i, j: (layer, 0, j)),
            pl.BlockSpec((tm, RET_KEY_DIM // 2), lambda i, j: (i % tiles_per_seq, 0)),
            pl.BlockSpec((tm, RET_KEY_DIM // 2), lambda i, j: (i % tiles_per_seq, 0)),
        ],
        out_specs=[
            pl.BlockSpec((tm, PROJ_BLOCK), lambda i, j: (i, jnp.minimum(j, BLK_RV))),
            pl.BlockSpec((tm, PROJ_BLOCK), lambda i, j: (i, 0)),
        ],
        out_shape=[
            jax.ShapeDtypeStruct((t, BLK_RG * PROJ_BLOCK), BF16),
            jax.ShapeDtypeStruct((t, PROJ_BLOCK), F32),
        ],
        scratch_shapes=[pltpu.VMEM((tm, d), BF16)],
        compiler_params=_params("arbitrary", "arbitrary"),
        name="in_proj",
    )(x, g, w, cos, sin)


def _t5_bucket(dist):
    max_exact = NUM_BUCKETS // 2
    is_small = dist < max_exact
    df = jnp.maximum(dist, 1).astype(F32)
    large = max_exact + (jnp.log(df / max_exact) / math.log(MAX_DISTANCE / max_exact)
                         * (NUM_BUCKETS - max_exact)).astype(jnp.int32)
    large = jnp.minimum(large, NUM_BUCKETS - 1)
    return jnp.where(is_small, dist, large)


def _attn_kernel(qi_ref, ki_ref, q_ref, k_ref, v_ref, bias_ref, lam_ref, subln_ref, o_ref,
                 acc_ref, m_ref, l_ref, band_ref, *, ta, lam_init):
    b = pl.program_id(0)
    p = pl.program_id(1)
    qi = qi_ref[p]
    ki = ki_ref[p]
    n_sub = ta // LANES

    @pl.when((b == 0) & (p == 0))
    def _():
        row = lax.broadcasted_iota(jnp.int32, (LANES, LANES), 0)
        col = lax.broadcasted_iota(jnp.int32, (LANES, LANES), 1)
        for slot in range(2):
            bucket = _t5_bucket(jnp.maximum(row - col + slot * LANES, 0))
            for m in range(N_DIFF_MAPS):
                val = jnp.zeros((LANES, LANES), F32)
                for bk in range(NUM_BUCKETS):
                    val = jnp.where(bucket == bk, bias_ref[bk, m], val)
                band_ref[slot, m] = (val - bias_ref[NUM_BUCKETS - 1, m]) * LOG2_E

    @pl.when(ki == 0)
    def _():
        m_ref[...] = jnp.full(m_ref.shape, NEG_INF, F32)
        l_ref[...] = jnp.zeros(l_ref.shape, F32)
        acc_ref[...] = jnp.zeros(acc_ref.shape, F32)

    def biased_slabs(s, m, kind):
        slabs = []
        for c in range(n_sub):
            sc = s[:, c * LANES:(c + 1) * LANES]
            if kind == "far" or (kind == "prev" and c != n_sub - 1):
                slabs.append(sc)
                continue
            pieces = []
            for a in range(n_sub):
                blk = sc[a * LANES:(a + 1) * LANES]
                if kind == "prev":
                    if a == 0:
                        blk = blk + band_ref[1, m]
                elif a < c:
                    blk = jnp.full((LANES, LANES), NEG_INF, F32)
                elif a == c:
                    row = lax.broadcasted_iota(jnp.int32, (LANES, LANES), 0)
                    col = lax.broadcasted_iota(jnp.int32, (LANES, LANES), 1)
                    blk = jnp.where(row >= col, blk + band_ref[0, m], NEG_INF)
                elif a == c + 1:
                    blk = blk + band_ref[1, m]
                pieces.append(blk)
            slabs.append(jnp.concatenate(pieces, axis=0))
        return slabs

    def scores(m):
        q_m = q_ref[:, m * DIFF_HEAD_DIM:(m + 1) * DIFF_HEAD_DIM]
        k_m = k_ref[:, m * DIFF_HEAD_DIM:(m + 1) * DIFF_HEAD_DIM]
        return lax.dot_general(q_m, k_m, (((1,), (1,)), ((), ())), preferred_element_type=F32)

    def step(kind):
        for m in range(N_DIFF_MAPS):
            s = scores(m)
            h = m // 2
            v_h = v_ref[:, h * DIFF_V_DIM:(h + 1) * DIFF_V_DIM]
            slabs = biased_slabs(s, m, kind)
            part_max = slabs[0]
            for sc in slabs[1:]:
                part_max = jnp.maximum(part_max, sc)
            m_prev = m_ref[m]
            m_new = jnp.maximum(m_prev, jnp.max(part_max, axis=1, keepdims=True))
            alpha = jnp.exp2(m_prev - m_new)
            probs = [jnp.exp2(sc - m_new) for sc in slabs]
            part_sum = probs[0]
            for pc in probs[1:]:
                part_sum = part_sum + pc
            m_ref[m] = m_new
            l_ref[m] = alpha * l_ref[m] + part_sum
            pv = jnp.dot(jnp.concatenate([pc.astype(BF16) for pc in probs], axis=1), v_h,
                         preferred_element_type=F32)
            alpha_v = jnp.concatenate([alpha] * (DIFF_V_DIM // LANES), axis=1)
            acc_ref[m] = acc_ref[m] * alpha_v + pv

    @pl.when(ki < qi - 1)
    def _():
        step("far")

    @pl.when((ki == qi - 1))
    def _():
        step("prev")

    @pl.when(ki == qi)
    def _():
        step("diag")
        lam_p = lam_ref[...]
        lam = (jnp.exp(jnp.sum(lam_p[0:1] * lam_p[1:2], axis=1, keepdims=True))
               - jnp.exp(jnp.sum(lam_p[2:3] * lam_p[3:4], axis=1, keepdims=True)) + lam_init)
        for h in range(N_DIFF_HEADS):
            inv0 = 1.0 / jnp.sum(l_ref[2 * h], axis=1, keepdims=True)
            inv1 = 1.0 / jnp.sum(l_ref[2 * h + 1], axis=1, keepdims=True)
            o = acc_ref[2 * h] * inv0 - lam * (acc_ref[2 * h + 1] * inv1)
            o = _rms_norm(o, subln_ref[...]) * (1.0 - lam_init)
            o_ref[:, h * DIFF_V_DIM:(h + 1) * DIFF_V_DIM] = o.astype(BF16)


def _diff_attention(proj, rel_bias, lam_params, subln, *, batch, seq, ta, lam_init):
    t = proj.shape[0]
    assert seq % ta == 0 and ta % LANES == 0 and ta >= 2 * LANES
    nq = seq // ta
    pairs = [(q, k) for q in range(nq) for k in range(q + 1)]
    qi = jnp.asarray([q for q, _ in pairs], jnp.int32)
    ki = jnp.asarray([k for _, k in pairs], jnp.int32)
    width = N_DIFF_HEADS * DIFF_V_DIM
    grid_spec = pltpu.PrefetchScalarGridSpec(
        num_scalar_prefetch=2,
        grid=(batch, len(pairs)),
        in_specs=[
            pl.BlockSpec((ta, PROJ_BLOCK), lambda b, p, qi, ki: (b * nq + qi[p], BLK_AQ)),
            pl.BlockSpec((ta, PROJ_BLOCK), lambda b, p, qi, ki: (b * nq + ki[p], BLK_AK)),
            pl.BlockSpec((ta, PROJ_BLOCK), lambda b, p, qi, ki: (b * nq + ki[p], BLK_AV)),
            pl.BlockSpec(memory_space=pltpu.SMEM),
            pl.BlockSpec((4, DIFF_HEAD_DIM), lambda b, p, qi, ki: (0, 0)),
            pl.BlockSpec((1, DIFF_V_DIM), lambda b, p, qi, ki: (0, 0)),
        ],
        out_specs=pl.BlockSpec((ta, width), lambda b, p, qi, ki: (b * nq + qi[p], 0)),
        scratch_shapes=[
            pltpu.VMEM((N_DIFF_MAPS, ta, DIFF_V_DIM), F32),
            pltpu.VMEM((N_DIFF_MAPS, ta, LANES), F32),
            pltpu.VMEM((N_DIFF_MAPS, ta, LANES), F32),
            pltpu.VMEM((2, N_DIFF_MAPS, LANES, LANES), F32),
        ],
    )
    return pl.pallas_call(
        functools.partial(_attn_kernel, ta=ta, lam_init=lam_init),
        grid_spec=grid_spec,
        out_shape=jax.ShapeDtypeStruct((t, width), BF16),
        compiler_params=_params("arbitrary", "arbitrary"),
        name="diff_attention",
    )(qi, ki, proj, proj, proj, rel_bias, lam_params, subln)


def _ret_log_decay(h):
    return math.log(1.0 - 2.0 ** (-5.0 - h))


def _retention_kernel(q_ref, k_ref, v_ref, g_ref, o_ref, state_ref, mask_ref, qdec_ref, kdec_ref, *, chunk):
    b = pl.program_id(0)
    c = pl.program_id(1)

    @pl.when((b == 0) & (c == 0))
    def _():
        rel = (lax.broadcasted_iota(jnp.int32, (chunk, chunk), 0)
               - lax.broadcasted_iota(jnp.int32, (chunk, chunk), 1)).astype(F32)
        idx = lax.broadcasted_iota(jnp.int32, (chunk, RET_KEY_DIM), 0).astype(F32)
        for h in range(N_RET_HEADS):
            log_g = _ret_log_decay(h)
            mask_ref[h] = jnp.where(rel >= 0, jnp.exp(jnp.maximum(rel, 0.0) * log_g), 0.0)
            qdec_ref[h] = jnp.exp((idx + 1.0) * log_g)
            kdec_ref[h] = jnp.exp((chunk - 1.0 - idx) * log_g)

    @pl.when(c == 0)
    def _():
        state_ref[...] = jnp.zeros(state_ref.shape, F32)

    for h in range(N_RET_HEADS):
        sl = slice(h * RET_KEY_DIM, (h + 1) * RET_KEY_DIM)
        q = q_ref[:, sl]
        k = k_ref[:, sl]
        v = v_ref[:, sl]
        scores = lax.dot_general(q, k, (((1,), (1,)), ((), ())), preferred_element_type=F32)
        inner = jnp.dot((scores * mask_ref[h]).astype(BF16), v, preferred_element_type=F32)
        state = state_ref[h]
        cross = jnp.dot(q, state.astype(BF16), preferred_element_type=F32) * qdec_ref[h]
        k_dec = (k.astype(F32) * kdec_ref[h]).astype(BF16)
        state_ref[h] = (state * math.exp(chunk * _ret_log_decay(h))
                        + lax.dot_general(k_dec, v, (((0,), (0,)), ((), ())), preferred_element_type=F32))
        o = inner + cross
        o = o * lax.rsqrt(jnp.mean(o * o, axis=-1, keepdims=True) + EPS)
        o_ref[:, sl] = (o * _silu(g_ref[:, sl])).astype(BF16)


def _retention(proj, gate, *, batch, seq, chunk):
    t = proj.shape[0]
    assert seq % chunk == 0
    nc = seq // chunk
    width = N_RET_HEADS * RET_V_DIM
    return pl.pallas_call(
        functools.partial(_retention_kernel, chunk=chunk),
        grid=(batch, nc),
        in_specs=[
            pl.BlockSpec((chunk, PROJ_BLOCK), lambda b, c: (b * nc + c, BLK_RQ)),
            pl.BlockSpec((chunk, PROJ_BLOCK), lambda b, c: (b * nc + c, BLK_RK)),
            pl.BlockSpec((chunk, PROJ_BLOCK), lambda b, c: (b * nc + c, BLK_RV)),
            pl.BlockSpec((chunk, PROJ_BLOCK), lambda b, c: (b * nc + c, 0)),
        ],
        out_specs=pl.BlockSpec((chunk, width), lambda b, c: (b * nc + c, 0)),
        out_shape=jax.ShapeDtypeStruct((t, width), BF16),
        scratch_shapes=[
            pltpu.VMEM((N_RET_HEADS, RET_KEY_DIM, RET_V_DIM), F32),
            pltpu.VMEM((N_RET_HEADS, chunk, chunk), F32),
            pltpu.VMEM((N_RET_HEADS, chunk, RET_KEY_DIM), F32),
            pltpu.VMEM((N_RET_HEADS, chunk, RET_KEY_DIM), F32),
        ],
        compiler_params=_params("arbitrary", "arbitrary"),
        name="retention",
    )(proj, proj, proj, gate)


def _out_proj_kernel(x_ref, a_ref, r_ref, w_ref, o_ref):
    ka = a_ref.shape[1]
    y = jnp.dot(a_ref[...], w_ref[:ka, :], preferred_element_type=F32)
    y = y + jnp.dot(r_ref[...], w_ref[ka:, :], preferred_element_type=F32)
    o_ref[...] = x_ref[...] + y


def _out_proj(x, a, r, w, layer, *, tm, tn):
    t, d = x.shape
    ka, kr = a.shape[1], r.shape[1]
    assert t % tm == 0 and d % tn == 0 and w.shape[1:] == (ka + kr, d)
    w_mode = pl.Buffered(1) if tn == d else None
    return pl.pallas_call(
        _out_proj_kernel,
        grid=(t // tm, d // tn),
        in_specs=[
            pl.BlockSpec((tm, tn), lambda i, j: (i, j)),
            pl.BlockSpec((tm, ka), lambda i, j: (i, 0)),
            pl.BlockSpec((tm, kr), lambda i, j: (i, 0)),
            pl.BlockSpec((None, ka + kr, tn), lambda i, j: (layer, 0, j), pipeline_mode=w_mode),
        ],
        out_specs=pl.BlockSpec((tm, tn), lambda i, j: (i, j)),
        out_shape=jax.ShapeDtypeStruct((t, d), F32),
        compiler_params=_params("arbitrary", "arbitrary"),
        name="out_proj",
    )(x, a, r, w)


def _pool_kernel(x_ref, xp_ref, g_ref, w_ref, scale_ref, o_ref, h_ref, *, tm, seq):
    i = pl.program_id(0)
    start = (i * tm) % seq
    g = g_ref[...]
    h_prev = _rms_norm(xp_ref[...], g)
    h_ref[0:POOL_HALO, :] = jnp.where(start == 0, 0.0, h_prev)
    h_ref[POOL_HALO:, :] = _rms_norm(x_ref[...], g)
    pos = start + lax.broadcasted_iota(jnp.int32, (tm, 1), 0)
    gd = w_ref.shape[1]
    for gi, win in enumerate(POOL_WINDOWS):
        cols = slice(gi * gd, (gi + 1) * gd)
        total = h_ref[:, cols]
        hg = total[POOL_HALO:]
        span = 1
        while span < win:
            total = total + pltpu.roll(total, span, 0)
            span *= 2
        inv_count = 1.0 / jnp.minimum(pos + 1, win).astype(F32)
        pooled = total[POOL_HALO:] * inv_count - hg
        y = jnp.dot(pooled.astype(BF16), w_ref[gi], preferred_element_type=F32)
        o_ref[:, cols] = x_ref[:, cols] + y * scale_ref[:, cols]


def _pool_mixer(x, g, w, layer, scale, *, seq, tm):
    t, d = x.shape
    assert t % tm == 0 and seq % tm == 0 and tm % POOL_HALO == 0 and max(POOL_WINDOWS) <= POOL_HALO
    assert all(win & (win - 1) == 0 for win in POOL_WINDOWS)
    halo_blocks = tm // POOL_HALO
    return pl.pallas_call(
        functools.partial(_pool_kernel, tm=tm, seq=seq),
        grid=(t // tm,),
        in_specs=[
            pl.BlockSpec((tm, d), lambda i: (i, 0)),
            pl.BlockSpec((POOL_HALO, d), lambda i: (jnp.maximum(i * halo_blocks - 1, 0), 0)),
            pl.BlockSpec((1, d), lambda i: (0, 0)),
            pl.BlockSpec((None,) + w.shape[1:], lambda i: (layer, 0, 0, 0)),
            pl.BlockSpec((1, d), lambda i: (0, 0)),
        ],
        out_specs=pl.BlockSpec((tm, d), lambda i: (i, 0)),
        out_shape=jax.ShapeDtypeStruct((t, d), F32),
        scratch_shapes=[pltpu.VMEM((tm + POOL_HALO, d), F32)],
        compiler_params=_params("arbitrary"),
        name="pool_mixer",
    )(x, x, g, w, scale)


def _conv(u, prev, cw_ref, cb_ref):
    ext = jnp.concatenate([prev, u], axis=0)
    y = cb_ref[...] + u * cw_ref[CONV_WIDTH - 1:CONV_WIDTH, :]
    for tap in range(1, CONV_WIDTH):
        shifted = pltpu.roll(ext, tap, 0)[CONV_HALO:]
        y = y + shifted * cw_ref[CONV_WIDTH - 1 - tap:CONV_WIDTH - tap, :]
    return y


def _ffn_kernel(*refs, tm, seq, final):
    if final:
        (x_ref, g_ref, wv_ref, wg_ref, cwv_ref, cwg_ref, cbv_ref, cbg_ref, wd_ref, gf_ref,
         o_ref, h_ref, tail_v_ref, tail_g_ref) = refs
    else:
        (x_ref, g_ref, wv_ref, wg_ref, cwv_ref, cwg_ref, cbv_ref, cbg_ref, wd_ref, wu_next_ref, wd_next_ref,
         o_ref, wu_cast_ref, wd_cast_ref, h_ref, tail_v_ref, tail_g_ref) = refs
    i = pl.program_id(0)
    f = pl.program_id(1)
    n_f = pl.num_programs(1)
    tsub = tm // FFN_SUB_TILES

    @pl.when(f == 0)
    def _():
        x = x_ref[...]
        h_ref[...] = _rms_norm(x, g_ref[...]).astype(BF16)
        o_ref[...] = x

    @pl.when((i * tm) % seq == 0)
    def _():
        tail_v_ref[f] = jnp.zeros(tail_v_ref.shape[1:], F32)
        tail_g_ref[f] = jnp.zeros(tail_g_ref.shape[1:], F32)

    def up(s):
        h_s = h_ref[s * tsub:(s + 1) * tsub, :]
        u_g = jnp.dot(h_s, wg_ref[...], preferred_element_type=F32)
        u_v = jnp.dot(h_s, wv_ref[...], preferred_element_type=F32)
        return (u_v, u_g)

    prev = (tail_v_ref[f], tail_g_ref[f])
    u_cur = up(0)
    for s in range(FFN_SUB_TILES):
        u_next = up(s + 1) if s + 1 < FFN_SUB_TILES else None
        gate = _silu(_conv(u_cur[1], prev[1], cwg_ref, cbg_ref))
        val = _conv(u_cur[0], prev[0], cwv_ref, cbv_ref)
        act = (gate * val).astype(BF16)
        o_ref[s * tsub:(s + 1) * tsub, :] += jnp.dot(act, wd_ref[...], preferred_element_type=F32)
        prev = (u_cur[0][tsub - CONV_HALO:], u_cur[1][tsub - CONV_HALO:])
        u_cur = u_next
    tail_v_ref[f] = prev[0]
    tail_g_ref[f] = prev[1]

    if final:
        @pl.when(f == n_f - 1)
        def _():
            o_ref[...] = _rms_norm(o_ref[...], gf_ref[...])
    else:
        wu_cast_ref[...] = wu_next_ref[...].astype(BF16)
        wd_cast_ref[...] = wd_next_ref[...].astype(BF16)


def _conv_ffn(x, g, w_up, conv_w, conv_b, w_down, layer, final_g, next_weights, *, seq, tm, tf):
    t, d = x.shape
    d_ff = w_down.shape[1]
    assert t % tm == 0 and seq % tm == 0 and d_ff % tf == 0 and w_up.shape[1:] == (d, 2 * d_ff)
    assert tm % FFN_SUB_TILES == 0 and (tm // FFN_SUB_TILES) % 16 == 0
    n_f = d_ff // tf
    steps = (t // tm) * n_f
    final = final_g is not None
    assert final != (next_weights is not None)
    in_specs = [
        pl.BlockSpec((tm, d), lambda i, f: (i, 0)),
        pl.BlockSpec((1, d), lambda i, f: (0, 0)),
        pl.BlockSpec((None, d, tf), lambda i, f: (layer, 0, f)),
        pl.BlockSpec((None, d, tf), lambda i, f: (layer, 0, n_f + f)),
        pl.BlockSpec((None, CONV_WIDTH, tf), lambda i, f: (layer, 0, f)),
        pl.BlockSpec((None, CONV_WIDTH, tf), lambda i, f: (layer, 0, n_f + f)),
        pl.BlockSpec((None, 1, tf), lambda i, f: (layer, 0, f)),
        pl.BlockSpec((None, 1, tf), lambda i, f: (layer, 0, n_f + f)),
        pl.BlockSpec((None, tf, d), lambda i, f: (layer, f, 0)),
    ]
    args = [x, g, w_up, w_up, conv_w, conv_w, conv_b, conv_b, w_down]
    out_specs = [pl.BlockSpec((tm, d), lambda i, f: (i, 0))]
    out_shape = [jax.ShapeDtypeStruct((t, d), F32)]
    if final:
        in_specs.append(pl.BlockSpec((1, d), lambda i, f: (0, 0)))
        args.append(final_g)
    else:
        w_up_next, w_down_next, next_layer = next_weights
        n_i = t // tm
        assert d % (16 * n_i) == 0 and (2 * d_ff) % (LANES * n_f) == 0 and d_ff % (16 * steps) == 0
        up_blk = (None, d // n_i, 2 * d_ff // n_f)
        down_blk = (None, d_ff // steps, d)
        in_specs += [
            pl.BlockSpec(up_blk, lambda i, f: (next_layer, i, f)),
            pl.BlockSpec(down_blk, lambda i, f: (next_layer, i * n_f + f, 0)),
        ]
        args += [w_up_next, w_down_next]
        out_specs += [
            pl.BlockSpec(up_blk, lambda i, f: (0, i, f)),
            pl.BlockSpec(down_blk, lambda i, f: (0, i * n_f + f, 0)),
        ]
        out_shape += [jax.ShapeDtypeStruct((1, d, 2 * d_ff), BF16), jax.ShapeDtypeStruct((1, d_ff, d), BF16)]
    outs = pl.pallas_call(
        functools.partial(_ffn_kernel, tm=tm, seq=seq, final=final),
        grid=(t // tm, n_f),
        in_specs=in_specs,
        out_specs=out_specs,
        out_shape=out_shape,
        scratch_shapes=[
            pltpu.VMEM((tm, d), BF16),
            pltpu.VMEM((n_f, CONV_HALO, tf), F32),
            pltpu.VMEM((n_f, CONV_HALO, tf), F32),
        ],
        compiler_params=_params("arbitrary", "arbitrary"),
        name="conv_ffn",
    )(*args)
    if final:
        return outs[0], None
    return outs[0], (outs[1], outs[2])


def _tiles(seq):
    pick = lambda want: min(want, seq)
    return dict(proj_tm=pick(1024), attn_ta=pick(512), ret_chunk=pick(512), out_tm=pick(512),
                pool_tm=pick(1024), ffn_tm=pick(1024))


def kernel(x, rel_bias, mix_norm, ffn_norm, final_norm, w_in, w_out, lam_q1, lam_k1, lam_q2, lam_k2,
           diff_subln, pool_w, pool_scale, w_up, conv_w, conv_b, w_down):
    batch, seq, d = x.shape
    depth = mix_norm.shape[0]
    tiles = _tiles(seq)
    xf = x.reshape(batch * seq, d)

    pos = jnp.arange(seq, dtype=F32)
    inv_freq = 1.0 / (ROPE_BASE ** jnp.linspace(0.0, 1.0, RET_KEY_DIM // 2, dtype=F32))
    ang = pos[:, None] * inv_freq[None, :]
    cos, sin = jnp.cos(ang), jnp.sin(ang)

    w_in_b, w_out_b, pool_w_b = w_in.astype(BF16), w_out.astype(BF16), pool_w.astype(BF16)
    ffn_w = (w_up[:1].astype(BF16), w_down[:1].astype(BF16))
    conv_b3 = conv_b[:, None, :]

    for l in range(depth):
        g_mix = mix_norm[l][None, :]
        if l % 2 == 0:
            i = l // 2
            lam_init = 0.8 - 0.6 * math.exp(-0.3 * l)
            proj, gate = _in_proj(xf, g_mix, w_in_b, i, cos, sin, seq=seq, tm=tiles["proj_tm"])
            lam_params = jnp.stack([lam_q1[i], lam_k1[i], lam_q2[i], lam_k2[i]]).astype(F32)
            a_out = _diff_attention(proj, rel_bias, lam_params, diff_subln[i][None, :], batch=batch, seq=seq,
                                    ta=tiles["attn_ta"], lam_init=lam_init)
            r_out = _retention(proj, gate, batch=batch, seq=seq, chunk=tiles["ret_chunk"])
            xf = _out_proj(xf, a_out, r_out, w_out_b, i, tm=tiles["out_tm"], tn=d)
        else:
            j = l // 2
            xf = _pool_mixer(xf, g_mix, pool_w_b, j, pool_scale[j][None, :], seq=seq, tm=tiles["pool_tm"])
        last = l == depth - 1
        xf, ffn_w = _conv_ffn(xf, ffn_norm[l][None, :], ffn_w[0], conv_w[l:l + 1], conv_b3[l:l + 1], ffn_w[1], 0,
                              final_norm[None, :] if last else None,
                              None if last else (w_up, w_down, l + 1), seq=seq, tm=tiles["ffn_tm"], tf=512)
    return xf.reshape(batch, seq, d)
```
